```python
import jax, jax.numpy as jnp
from jax import lax
import numpy as np

D_MODEL = 1024
BATCH = 2
SEQ = 16384
DEPTH = 4

CONV_DIM = 512
CONV_WIDTH = 31
SC_DIM = 512
SC_WIDTH = 3
MLA_HEADS = 8
QK_NOPE_DIM = 64
QK_ROPE_DIM = 32
V_HEAD_DIM = 64
Q_LORA_RANK = 256
KV_LORA_RANK = 128
ROPE_THETA = 10000.0
ATTN_BLOCK = 128
POOL_WINDOWS = (2, 4, 8, 16)
POOL_GROUPS = 4
POOL_DIM = 512
POOL_GROUP_DIM = POOL_DIM // POOL_GROUPS
N_BRANCHES = 4
D_FF = 2816
FFN_CONV_WIDTH = 3
LN_EPS = 1e-5
RMS_EPS = 1e-6
DEEPNORM_ALPHA = (2.0 * DEPTH) ** 0.25
DEEPNORM_BETA = (8.0 * DEPTH) ** -0.25

OFF_CONV_A = CONV_DIM
OFF_CONV_B = OFF_CONV_A + CONV_DIM
OFF_SC_B = OFF_CONV_B + SC_DIM
OFF_SC_C = OFF_SC_B + SC_DIM
OFF_SC_X = OFF_SC_C + SC_DIM
OFF_Q_LAT = OFF_SC_X + Q_LORA_RANK
OFF_KV_LAT = OFF_Q_LAT + KV_LORA_RANK
OFF_K_ROPE = OFF_KV_LAT + QK_ROPE_DIM
OFF_POOL = OFF_K_ROPE + POOL_DIM
IN_COLS = OFF_POOL + N_BRANCHES * D_MODEL
IN_OFFSETS = (OFF_CONV_A, OFF_CONV_B, OFF_SC_B, OFF_SC_C, OFF_SC_X, OFF_Q_LAT, OFF_KV_LAT, OFF_K_ROPE, OFF_POOL)

kernel_name = 'hybrid_gated_conv_mla_pool_deepnorm'


def layer_norm(x, g, b):
    xf = x.astype(jnp.float32)
    mu = jnp.mean(xf, axis=-1, keepdims=True)
    var = jnp.mean(jnp.square(xf - mu), axis=-1, keepdims=True)
    y = (xf - mu) * lax.rsqrt(var + LN_EPS)
    return (y * g.astype(jnp.float32) + b.astype(jnp.float32)).astype(x.dtype)


def rms_norm(x, g):
    xf = x.astype(jnp.float32)
    y = xf * lax.rsqrt(jnp.mean(jnp.square(xf), axis=-1, keepdims=True) + RMS_EPS)
    return (y * g.astype(jnp.float32)).astype(x.dtype)


def causal_dwconv(x, w):
    k, c = w.shape
    return lax.conv_general_dilated(
        x, w[:, None, :].astype(x.dtype), window_strides=(1,), padding=[(k - 1, 0)],
        dimension_numbers=('NWC', 'WIO', 'NWC'), feature_group_count=c)


def rope_tables(positions):
    inv = 1.0 / (ROPE_THETA ** (jnp.arange(0, QK_ROPE_DIM, 2, dtype=jnp.float32) / QK_ROPE_DIM))
    ang = positions.astype(jnp.float32)[..., None] * inv
    return jnp.cos(ang), jnp.sin(ang)


def apply_rope(x, cos, sin):
    half = QK_ROPE_DIM // 2
    xf = x.astype(jnp.float32)
    x1, x2 = xf[..., :half], xf[..., half:]
    return jnp.concatenate([x1 * cos - x2 * sin, x2 * cos + x1 * sin], axis=-1).astype(x.dtype)


def mla_attention(q_nope, q_rope, k_nope, k_rope, v):
    b, s, h, _ = q_nope.shape
    nb = s // ATTN_BLOCK
    scale = (QK_NOPE_DIM + QK_ROPE_DIM) ** -0.5
    key_idx = jnp.arange(s)

    def to_blocks(t):
        return jnp.moveaxis(t.reshape(b, nb, ATTN_BLOCK, *t.shape[2:]), 1, 0)

    def one_block(args):
        qn, qr, blk = args
        sc = (jnp.einsum('bqhd,bkhd->bhqk', qn, k_nope, preferred_element_type=jnp.float32)
              + jnp.einsum('bqhd,bkd->bhqk', qr, k_rope, preferred_element_type=jnp.float32))
        q_idx = blk * ATTN_BLOCK + jnp.arange(ATTN_BLOCK)
        mask = key_idx[None, :] <= q_idx[:, None]
        p = jax.nn.softmax(jnp.where(mask, sc * scale, -jnp.inf), axis=-1)
        return jnp.einsum('bhqk,bkhd->bqhd', p.astype(v.dtype), v)

    out = lax.map(one_block, (to_blocks(q_nope), to_blocks(q_rope), jnp.arange(nb)))
    return jnp.moveaxis(out, 0, 1).reshape(b, s, h * V_HEAD_DIM)


def multiscale_pool(u):
    s = u.shape[1]
    uf = u.astype(jnp.float32)
    cs = jnp.cumsum(uf, axis=1)
    t = jnp.arange(s)
    means = []
    for g, w in enumerate(POOL_WINDOWS):
        cg = cs[..., g * POOL_GROUP_DIM:(g + 1) * POOL_GROUP_DIM]
        prev = jnp.pad(cg, ((0, 0), (w, 0), (0, 0)))[:, :s]
        cnt = jnp.minimum(t + 1, w).astype(jnp.float32)[None, :, None]
        means.append((cg - prev) / cnt)
    return (jnp.concatenate(means, axis=-1) - uf).astype(u.dtype)


def setup_inputs(seed: int = 0) -> dict:
    key = jax.random.key(seed)
    ks = iter(jax.random.split(key, 40))
    L, D = DEPTH, D_MODEL
    beta = DEEPNORM_BETA

    def nrm(shape, scale):
        return scale * jax.random.normal(next(ks), shape, jnp.float32)

    x = nrm((BATCH, SEQ, D), 1.0)
    c = nrm((BATCH, D), 1.0)
    offsets = jax.random.randint(next(ks), (BATCH, 1), 0, 4096, dtype=jnp.int32)
    positions = offsets + jnp.arange(SEQ, dtype=jnp.int32)[None, :]
    return {
        'x': x,
        'c': c,
        'positions': positions,
        'w_ada': nrm((L, D, 6 * D), 0.1 * D ** -0.5),
        'b_ada': nrm((L, 6 * D), 0.02),
        'w_in': nrm((L, D, IN_COLS), D ** -0.5),
        'b_in': nrm((L, IN_COLS), 0.02),
        'conv_dw': nrm((L, CONV_WIDTH, CONV_DIM), CONV_WIDTH ** -0.5),
        'conv_ln_g': 1.0 + nrm((L, CONV_DIM), 0.1),
        'conv_ln_b': nrm((L, CONV_DIM), 0.02),
        'w_conv_out': nrm((L, CONV_DIM, D), beta * CONV_DIM ** -0.5),
        'sc_dw': nrm((L, SC_WIDTH, SC_DIM), SC_WIDTH ** -0.5),
        'w_sc_out': nrm((L, SC_DIM, D), beta * SC_DIM ** -0.5),
        'q_norm_g': 1.0 + nrm((L, Q_LORA_RANK), 0.1),
        'w_uq': nrm((L, Q_LORA_RANK, MLA_HEADS * (QK_NOPE_DIM + QK_ROPE_DIM)), Q_LORA_RANK ** -0.5),
        'kv_norm_g': 1.0 + nrm((L, KV_LORA_RANK), 0.1),
        'w_ukv': nrm((L, KV_LORA_RANK, MLA_HEADS * (QK_NOPE_DIM + V_HEAD_DIM)), KV_LORA_RANK ** -0.5),
        'w_mla_out': nrm((L, MLA_HEADS * V_HEAD_DIM, D), beta * (MLA_HEADS * V_HEAD_DIM) ** -0.5),
        'w_pool': nrm((L, POOL_GROUPS, POOL_GROUP_DIM, POOL_GROUP_DIM), POOL_GROUP_DIM ** -0.5),
        'pool_scale': 1.0 + nrm((L, POOL_DIM), 0.1),
        'w_pool_out': nrm((L, POOL_DIM, D), beta * POOL_DIM ** -0.5),
        'w_o': nrm((L, D, D), beta * D ** -0.5),
        'ln1_g': 1.0 + nrm((L, D), 0.1),
        'ln1_b': nrm((L, D), 0.02),
        'w_up': nrm((L, D, 2 * D_FF), D ** -0.5),
        'ffn_dw': nrm((L, FFN_CONV_WIDTH, 2 * D_FF), FFN_CONV_WIDTH ** -0.5),
        'w_down': nrm((L, D_FF, D), beta * D_FF ** -0.5),
        'ln2_g': 1.0 + nrm((L, D), 0.1),
        'ln2_b': nrm((L, D), 0.02),
    }


def reference(x, c, positions, w_ada, b_ada, w_in, b_in, conv_dw, conv_ln_g, conv_ln_b, w_conv_out,
              sc_dw, w_sc_out, q_norm_g, w_uq, kv_norm_g, w_ukv, w_mla_out, w_pool, pool_scale,
              w_pool_out, w_o, ln1_g, ln1_b, w_up, ffn_dw, w_down, ln2_g, ln2_b):
    b, s, d = x.shape
    cos, sin = rope_tables(positions)
    c_act = jax.nn.silu(c)
    for l in range(DEPTH):
        mod = (c_act @ w_ada[l] + b_ada[l])[:, None, :]
        sh1, sc1, g1, sh2, sc2, g2 = jnp.split(mod, 6, axis=-1)

        h = x * (1.0 + sc1) + sh1
        proj = h @ w_in[l] + b_in[l]
        conv_a, conv_b, sc_bg, sc_cg, sc_x, q_lat, kv_lat, k_rope, pool_u, gates = jnp.split(
            proj, IN_OFFSETS, axis=-1)

        ya = conv_a * jax.nn.sigmoid(conv_b)
        ya = causal_dwconv(ya, conv_dw[l])
        ya = jax.nn.silu(layer_norm(ya, conv_ln_g[l], conv_ln_b[l]))
        ya = ya @ w_conv_out[l]

        yb = (sc_bg * causal_dwconv(sc_cg * sc_x, sc_dw[l])) @ w_sc_out[l]

        q = (rms_norm(q_lat, q_norm_g[l]) @ w_uq[l]).reshape(b, s, MLA_HEADS, QK_NOPE_DIM + QK_ROPE_DIM)
        q_nope = q[..., :QK_NOPE_DIM]
        q_rope = apply_rope(q[..., QK_NOPE_DIM:], cos[:, :, None, :], sin[:, :, None, :])
        kv = (rms_norm(kv_lat, kv_norm_g[l]) @ w_ukv[l]).reshape(b, s, MLA_HEADS, QK_NOPE_DIM + V_HEAD_DIM)
        k_nope, v = kv[..., :QK_NOPE_DIM], kv[..., QK_NOPE_DIM:]
        k_rope_r = apply_rope(k_rope, cos, sin)
        yc = mla_attention(q_nope, q_rope, k_nope, k_rope_r, v) @ w_mla_out[l]

        pd = multiscale_pool(pool_u).reshape(b, s, POOL_GROUPS, POOL_GROUP_DIM)
        yd = jnp.einsum('bsgc,gcd->bsgd', pd, w_pool[l]).reshape(b, s, POOL_DIM) * pool_scale[l]
        yd = yd @ w_pool_out[l]

        gt = jax.nn.sigmoid(gates.astype(jnp.float32)).astype(x.dtype).reshape(b, s, N_BRANCHES, d)
        merged = gt[:, :, 0] * ya + gt[:, :, 1] * yb + gt[:, :, 2] * yc + gt[:, :, 3] * yd
        mix = merged @ w_o[l]
        x = layer_norm(DEEPNORM_ALPHA * x + (1.0 + g1) * mix, ln1_g[l], ln1_b[l])

        h = x * (1.0 + sc2) + sh2
        up = causal_dwconv(h @ w_up[l], ffn_dw[l])
        val, gate = jnp.split(up, 2, axis=-1)
        ffn = (jax.nn.silu(gate) * val) @ w_down[l]
        x = layer_norm(DEEPNORM_ALPHA * x + (1.0 + g2) * ffn, ln2_g[l], ln2_b[l])
    return x
```

```python
import functools

import jax
import jax.numpy as jnp
from jax import lax
from jax.experimental import pallas as pl
from jax.experimental.pallas import tpu as pltpu

F32 = jnp.float32
BF16 = jnp.bfloat16

D_MODEL = 1024
DEPTH = 4
CONV_DIM = 512
CONV_WIDTH = 31
SC_DIM = 512
SC_WIDTH = 3
MLA_HEADS = 8
QK_NOPE_DIM = 64
QK_ROPE_DIM = 32
V_HEAD_DIM = 64
Q_LORA_RANK = 256
KV_LORA_RANK = 128
ROPE_THETA = 10000.0
POOL_WINDOWS = (2, 4, 8, 16)
POOL_GROUPS = 4
POOL_DIM = 512
POOL_GROUP_DIM = POOL_DIM // POOL_GROUPS
N_BRANCHES = 4
D_FF = 2816
FFN_CONV_WIDTH = 3
LN_EPS = 1e-5
RMS_EPS = 1e-6
DEEPNORM_ALPHA = (2.0 * DEPTH) ** 0.25

OFF_CONV_A = CONV_DIM
OFF_CONV_B = OFF_CONV_A + CONV_DIM
OFF_SC_B = OFF_CONV_B + SC_DIM
OFF_SC_C = OFF_SC_B + SC_DIM
OFF_SC_X = OFF_SC_C + SC_DIM
OFF_Q_LAT = OFF_SC_X + Q_LORA_RANK
OFF_KV_LAT = OFF_Q_LAT + KV_LORA_RANK
OFF_K_ROPE = OFF_KV_LAT + QK_ROPE_DIM
OFF_POOL = OFF_K_ROPE + POOL_DIM

LANES = 128
SUBLANES = 8
HEAD_PAD = LANES
ROPE_HALF = QK_ROPE_DIM // 2
ATTN_SCALE = (QK_NOPE_DIM + QK_ROPE_DIM) ** -0.5
MASK_VALUE = -1e30

HIST_A = 32
HIST_B = 8
HIST_P = 16
HIST_F = 8

FFN_CHUNK = 256
FFN_NCHUNK = D_FF // FFN_CHUNK
VMEM_LIMIT = 56 * 1024 * 1024


def _dot(a, b):
    return jnp.dot(a, b, preferred_element_type=F32)


def _sigmoid(x):
    return jax.nn.sigmoid(x)


def _silu(x):
    return x * jax.nn.sigmoid(x)


def _layer_norm(x, g, b):
    mu = jnp.mean(x, axis=-1, keepdims=True)
    xc = x - mu
    var = jnp.mean(xc * xc, axis=-1, keepdims=True)
    return xc * lax.rsqrt(var + LN_EPS) * g + b


def _rms_norm(x, g):
    return x * lax.rsqrt(jnp.mean(x * x, axis=-1, keepdims=True) + RMS_EPS) * g


def _const_spec(shape):
    n = len(shape)
    return pl.BlockSpec(shape, lambda *_: (0,) * n, pipeline_mode=pl.Buffered(1))


def _params(n_grid):
    return pltpu.CompilerParams(dimension_semantics=("arbitrary",) * n_grid, vmem_limit_bytes=VMEM_LIMIT)


def _mod_kernel(c_ref, w_ref, b_ref, o_ref):
    c = c_ref[...]
    o_ref[...] = _dot(_silu(c).astype(BF16), w_ref[...].astype(BF16)) + b_ref[...]


def _modulation(c, w_ada, b_ada):
    n_layers, d, cols = w_ada.shape
    b = c.shape[0]
    rows = SUBLANES
    nc = 4
    cw = cols // nc
    c_pad = jnp.zeros((rows, d), F32).at[:b].set(c)
    out = pl.pallas_call(
        _mod_kernel,
        grid=(n_layers, nc),
        in_specs=[
            pl.BlockSpec((rows, d), lambda l, j: (0, 0)),
            pl.BlockSpec((None, d, cw), lambda l, j: (l, 0, j)),
            pl.BlockSpec((None, 1, cw), lambda l, j: (l, 0, j)),
        ],
        out_specs=pl.BlockSpec((None, rows, cw), lambda l, j: (l, 0, j)),
        out_shape=jax.ShapeDtypeStruct((n_layers, rows, cols), F32),
        compiler_params=_params(2),
        name="adaln_mod",
    )(c_pad, w_ada, b_ada.reshape(n_layers, 1, cols))
    return out[:, :b].reshape(n_layers, b, 6, d)


def _causal_taps(buf, taps_ref, hist, ts):
    k_taps = taps_ref.shape[0]
    acc = None
    for k in range(k_taps):
        term = taps_ref[k:k + 1, :] * buf[pl.ds(hist - (k_taps - 1) + k, ts), :]
        acc = term if acc is None else acc + term
    return acc


def _local_kernel(x_ref, mod_ref, w1_ref, b1_ref, wg_ref, bg_ref, cdw_ref, lng_ref, lnb_ref, wco_ref,
                  sdw_ref, wso_ref, wp_ref, ps_ref, wpo_ref, part_ref, buf_a, buf_b, buf_p, *, ts):
    s_idx = pl.program_id(1)

    @pl.when(s_idx == 0)
    def _():
        buf_a[0:HIST_A, :] = jnp.zeros((HIST_A, CONV_DIM), F32)
        buf_b[0:HIST_B, :] = jnp.zeros((HIST_B, SC_DIM), F32)
        buf_p[0:HIST_P, :] = jnp.zeros((HIST_P, POOL_DIM), F32)

    mod = mod_ref[...]
    h = (x_ref[...] * (1.0 + mod[1:2, :]) + mod[0:1, :]).astype(BF16)

    def proj(lo, hi):
        return _dot(h, w1_ref[:, lo:hi]) + b1_ref[:, lo:hi]

    buf_a[HIST_A:HIST_A + ts, :] = proj(0, 512) * _sigmoid(proj(512, 1024))
    ya = _causal_taps(buf_a, cdw_ref, HIST_A, ts)
    buf_a[0:HIST_A, :] = buf_a[ts:ts + HIST_A, :]
    ya = _silu(_layer_norm(ya, lng_ref[...], lnb_ref[...]))
    ya = _dot(ya.astype(BF16), wco_ref[...])
    part = _sigmoid(_dot(h, wg_ref[:, 0:D_MODEL]) + bg_ref[:, 0:D_MODEL]) * ya

    sc_b = proj(1024, 1536)
    buf_b[HIST_B:HIST_B + ts, :] = proj(1536, 2048) * proj(2048, 2560)
    yb = sc_b * _causal_taps(buf_b, sdw_ref, HIST_B, ts)
    buf_b[0:HIST_B, :] = buf_b[ts:ts + HIST_B, :]
    yb = _dot(yb.astype(BF16), wso_ref[...])
    part += _sigmoid(_dot(h, wg_ref[:, D_MODEL:2 * D_MODEL]) + bg_ref[:, D_MODEL:2 * D_MODEL]) * yb

    u = proj(2560, 3072)
    buf_p[HIST_P:HIST_P + ts, :] = u
    pos = s_idx * ts + lax.broadcasted_iota(jnp.int32, (ts, 1), 0)
    groups = []
    for g, w in enumerate(POOL_WINDOWS):
        lo, hi = g * POOL_GROUP_DIM, (g + 1) * POOL_GROUP_DIM
        tot = u[:, lo:hi]
        for j in range(1, w):
            tot = tot + buf_p[pl.ds(HIST_P - j, ts), lo:hi]
        cnt = jnp.minimum(pos + 1, w).astype(F32)
        pd = tot / cnt - u[:, lo:hi]
        groups.append(_dot(pd.astype(BF16), wp_ref[g]))
    buf_p[0:HIST_P, :] = buf_p[ts:ts + HIST_P, :]
    yd = jnp.concatenate(groups, axis=-1) * ps_ref[...]
    yd = _dot(yd.astype(BF16), wpo_ref[...])
    part += _sigmoid(_dot(h, wg_ref[:, 2 * D_MODEL:3 * D_MODEL]) + bg_ref[:, 2 * D_MODEL:3 * D_MODEL]) * yd

    part_ref[...] = part


def _local_mixers(x, mod_l, lw, ts):
    b, s, d = x.shape
    weights = [lw["w1"], lw["b1"], lw["wg"], lw["bg"], lw["conv_dw"], lw["conv_ln_g"], lw["conv_ln_b"],
               lw["w_conv_out"], lw["sc_dw"], lw["w_sc_out"], lw["w_pool"], lw["pool_scale"], lw["w_pool_out"]]
    return pl.pallas_call(
        functools.partial(_local_kernel, ts=ts),
        grid=(b, s // ts),
        in_specs=[
            pl.BlockSpec((None, ts, d), lambda i, j: (i, j, 0)),
            pl.BlockSpec((None, 6, d), lambda i, j: (i, 0, 0)),
        ] + [_const_spec(w.shape) for w in weights],
        out_specs=pl.BlockSpec((None, ts, d), lambda i, j: (i, j, 0)),
        out_shape=jax.ShapeDtypeStruct((b, s, d), F32),
        scratch_shapes=[
            pltpu.VMEM((HIST_A + ts, CONV_DIM), F32),
            pltpu.VMEM((HIST_B + ts, SC_DIM), F32),
            pltpu.VMEM((HIST_P + ts, POOL_DIM), F32),
        ],
        compiler_params=_params(2),
        name="local_mixers",
    )(x, mod_l, *weights)


def _mla_proj_kernel(x_ref, mod_ref, cos_ref, sin_ref, w2_ref, b2_ref, qg_ref, kvg_ref, wq_ref, wqs_ref,
                     wk_ref, wv_ref, vone_ref, q_ref, k_ref, v_ref):
    mod = mod_ref[...]
    h = (x_ref[...] * (1.0 + mod[1:2, :]) + mod[0:1, :]).astype(BF16)
    lat = _dot(h, w2_ref[...]) + b2_ref[...]
    o_kv = Q_LORA_RANK
    o_kr = o_kv + KV_LORA_RANK
    qn = _rms_norm(lat[:, 0:o_kv], qg_ref[...]).astype(BF16)
    kvn = _rms_norm(lat[:, o_kv:o_kr], kvg_ref[...]).astype(BF16)
    cos = cos_ref[...]
    sin = sin_ref[...]
    k_rot = lat[:, o_kr:o_kr + LANES] * cos + lat[:, o_kr + LANES:o_kr + 2 * LANES] * sin
    qa = _dot(qn, wq_ref[...])
    qb = _dot(qn, wqs_ref[...])
    kf = _dot(kvn, wk_ref[...])
    vf = _dot(kvn, wv_ref[...]) + vone_ref[...]
    for hh in range(MLA_HEADS):
        lo, hi = hh * HEAD_PAD, (hh + 1) * HEAD_PAD
        q_ref[hh] = ((qa[:, lo:hi] * cos + qb[:, lo:hi] * sin) * ATTN_SCALE).astype(BF16)
        k_ref[hh] = (kf[:, lo:hi] + k_rot).astype(BF16)
        v_ref[hh] = vf[:, lo:hi].astype(BF16)


def _mla_proj(x, mod_l, cos_t, sin_t, lw, ts):
    b, s, d = x.shape
    weights = [lw["w2"], lw["b2"], lw["q_norm_g"], lw["kv_norm_g"], lw["wq"], lw["wq_swap"], lw["wk"], lw["wv"],
               lw["v_one"]]
    head_spec = pl.BlockSpec((None, MLA_HEADS, ts, HEAD_PAD), lambda i, j: (i, 0, j, 0))
    head_shape = jax.ShapeDtypeStruct((b, MLA_HEADS, s, HEAD_PAD), BF16)
    return pl.pallas_call(
        _mla_proj_kernel,
        grid=(b, s // ts),
        in_specs=[
            pl.BlockSpec((None, ts, d), lambda i, j: (i, j, 0)),
            pl.BlockSpec((None, 6, d), lambda i, j: (i, 0, 0)),
            pl.BlockSpec((None, ts, LANES), lambda i, j: (i, j, 0)),
            pl.BlockSpec((None, ts, LANES), lambda i, j: (i, j, 0)),
        ] + [_const_spec(w.shape) for w in weights],
        out_specs=[head_spec, head_spec, head_spec],
        out_shape=[head_shape, head_shape, head_shape],
        compiler_params=_params(2),
        name="mla_proj",
    )(x, mod_l, cos_t, sin_t, *weights)


def _flash_kernel(qi_ref, ki_ref, q_ref, k_ref, v_ref, o_ref, m_ref, acc_ref, *, tq):
    t = pl.program_id(1)
    qi = qi_ref[t]
    ki = ki_ref[t]

    @pl.when(ki == 0)
    def _():
        m_ref[...] = jnp.full(m_ref.shape, MASK_VALUE, F32)
        acc_ref[...] = jnp.zeros(acc_ref.shape, F32)

    def block(masked):
        def head(hh, carry):
            s = lax.dot_general(q_ref[hh], k_ref[hh], (((1,), (1,)), ((), ())), preferred_element_type=F32)
            if masked:
                row = lax.broadcasted_iota(jnp.int32, (tq, tq), 0)
                col = lax.broadcasted_iota(jnp.int32, (tq, tq), 1)
                s = jnp.where(col <= row, s, MASK_VALUE)
            m_old = m_ref[hh]
            m_new = jnp.maximum(m_old, jnp.max(s, axis=1, keepdims=True))
            alpha = jnp.exp(m_old - m_new)
            p = jnp.exp(s - m_new[:, 0:1])
            acc_ref[hh] = alpha * acc_ref[hh] + _dot(p.astype(BF16), v_ref[hh])
            m_ref[hh] = m_new
            return carry
        lax.fori_loop(0, MLA_HEADS, head, 0)

    @pl.when(ki < qi)
    def _():
        block(False)

    @pl.when(ki == qi)
    def _():
        block(True)

        def finish(hh, carry):
            acc = acc_ref[hh]
            o_ref[hh] = (acc / acc[:, V_HEAD_DIM:V_HEAD_DIM + 1]).astype(BF16)
            return carry
        lax.fori_loop(0, MLA_HEADS, finish, 0)


def _flash_attention(q, k, v, tq):
    b, nh, s, hp = q.shape
    nq = s // tq
    pairs = [(i, j) for i in range(nq) for j in range(i + 1)]
    qi_tab = jnp.asarray([p[0] for p in pairs], jnp.int32)
    ki_tab = jnp.asarray([p[1] for p in pairs], jnp.int32)
    q_spec = pl.BlockSpec((None, nh, tq, hp), lambda i, t, qi, ki: (i, 0, qi[t], 0))
    kv_spec = pl.BlockSpec((None, nh, tq, hp), lambda i, t, qi, ki: (i, 0, ki[t], 0))
    return pl.pallas_call(
        functools.partial(_flash_kernel, tq=tq),
        grid_spec=pltpu.PrefetchScalarGridSpec(
            num_scalar_prefetch=2,
            grid=(b, len(pairs)),
            in_specs=[q_spec, kv_spec, kv_spec],
            out_specs=q_spec,
            scratch_shapes=[
                pltpu.VMEM((nh, tq, LANES), F32),
                pltpu.VMEM((nh, tq, hp), F32),
            ],
        ),
        out_shape=jax.ShapeDtypeStruct((b, nh, s, hp), BF16),
        compiler_params=_params(2),
        name="flash_attention",
    )(qi_tab, ki_tab, q, k, v)


def _merge_kernel(x_ref, mod_ref, part_ref, o_ref, wg_ref, bg_ref, wmo_ref, wo_ref, g_ref, b_ref, out_ref):
    mod = mod_ref[...]
    x = x_ref[...]
    h = (x * (1.0 + mod[1:2, :]) + mod[0:1, :]).astype(BF16)
    gate = _sigmoid(_dot(h, wg_ref[...]) + bg_ref[...])
    yc = None
    for hh in range(MLA_HEADS):
        term = _dot(o_ref[hh], wmo_ref[hh])
        yc = term if yc is None else yc + term
    merged = part_ref[...] + gate * yc
    mix = _dot(merged.astype(BF16), wo_ref[...])
    out_ref[...] = _layer_norm(DEEPNORM_ALPHA * x + (1.0 + mod[2:3, :]) * mix, g_ref[...], b_ref[...])


def _merge(x, mod_l, part, attn, lw, ts):
    b, s, d = x.shape
    weights = [lw["wg_c"], lw["bg_c"], lw["w_mla_out"], lw["w_o"], lw["ln1_g"], lw["ln1_b"]]
    tile = pl.BlockSpec((None, ts, d), lambda i, j: (i, j, 0))
    return pl.pallas_call(
        _merge_kernel,
        grid=(b, s // ts),
        in_specs=[
            tile,
            pl.BlockSpec((None, 6, d), lambda i, j: (i, 0, 0)),
            tile,
            pl.BlockSpec((None, MLA_HEADS, ts, HEAD_PAD), lambda i, j: (i, 0, j, 0)),
        ] + [_const_spec(w.shape) for w in weights],
        out_specs=tile,
        out_shape=jax.ShapeDtypeStruct((b, s, d), F32),
        compiler_params=_params(2),
        name="merge_ln",
    )(x, mod_l, part, attn, *weights)


def _ffn_kernel(x_ref, mod_ref, wv_ref, wgt_ref, dwv_ref, dwg_ref, wd_ref, g_ref, b_ref, out_ref,
                hist_v, hist_g, buf_v, buf_g, acc_ref, *, ts):
    s_idx = pl.program_id(1)

    @pl.when(s_idx == 0)
    def _():
        hist_v[...] = jnp.zeros(hist_v.shape, F32)
        hist_g[...] = jnp.zeros(hist_g.shape, F32)

    mod = mod_ref[...]
    x = x_ref[...]
    h = (x * (1.0 + mod[4:5, :]) + mod[3:4, :]).astype(BF16)
    acc_ref[...] = jnp.zeros(acc_ref.shape, F32)

    def conv(w_ref, dw_ref, hist, buf, c):
        buf[0:HIST_F, :] = hist[c]
        buf[HIST_F:HIST_F + ts, :] = _dot(h, w_ref[c])
        hist[c] = buf[ts:ts + HIST_F, :]
        taps = dw_ref[c]
        out = None
        for k in range(FFN_CONV_WIDTH):
            term = taps[k:k + 1, :] * buf[pl.ds(HIST_F - (FFN_CONV_WIDTH - 1) + k, ts), :]
            out = term if out is None else out + term
        return out

    def chunk(c, carry):
        val = conv(wv_ref, dwv_ref, hist_v, buf_v, c)
        gate = conv(wgt_ref, dwg_ref, hist_g, buf_g, c)
        acc_ref[...] += _dot((_silu(gate) * val).astype(BF16), wd_ref[c])
        return carry

    lax.fori_loop(0, FFN_NCHUNK, chunk, 0)
    out_ref[...] = _layer_norm(DEEPNORM_ALPHA * x + (1.0 + mod[5:6, :]) * acc_ref[...], g_ref[...], b_ref[...])


def _ffn(x, mod_l, lw, ts):
    b, s, d = x.shape
    weights = [lw["w_up_v"], lw["w_up_g"], lw["ffn_dw_v"], lw["ffn_dw_g"], lw["w_down"], lw["ln2_g"], lw["ln2_b"]]
    tile = pl.BlockSpec((None, ts, d), lambda i, j: (i, j, 0))
    return pl.pallas_call(
        functools.partial(_ffn_kernel, ts=ts),
        grid=(b, s // ts),
        in_specs=[tile, pl.BlockSpec((None, 6, d), lambda i, j: (i, 0, 0))]
        + [_const_spec(w.shape) for w in weights],
        out_specs=tile,
        out_shape=jax.ShapeDtypeStruct((b, s, d), F32),
        scratch_shapes=[
            pltpu.VMEM((FFN_NCHUNK, HIST_F, FFN_CHUNK), F32),
            pltpu.VMEM((FFN_NCHUNK, HIST_F, FFN_CHUNK), F32),
            pltpu.VMEM((HIST_F + ts, FFN_CHUNK), F32),
            pltpu.VMEM((HIST_F + ts, FFN_CHUNK), F32),
            pltpu.VMEM((ts, d), F32),
        ],
        compiler_params=_params(2),
        name="conv_glu_ffn",
    )(x, mod_l, *weights)


def _rope_tables(positions):
    inv = 1.0 / (ROPE_THETA ** (jnp.arange(0, QK_ROPE_DIM, 2, dtype=F32) / QK_ROPE_DIM))
    ang = positions.astype(F32)[..., None] * inv
    cos, sin = jnp.cos(ang), jnp.sin(ang)
    lead = cos.shape[:-1]
    pad = HEAD_PAD - QK_NOPE_DIM - QK_ROPE_DIM
    cos_t = jnp.concatenate([jnp.ones(lead + (QK_NOPE_DIM,), F32), cos, cos, jnp.ones(lead + (pad,), F32)], -1)
    sin_t = jnp.concatenate([jnp.zeros(lead + (QK_NOPE_DIM,), F32), sin, sin, jnp.zeros(lead + (pad,), F32)], -1)
    return cos_t, sin_t


def _rope_columns(w):
    rows = w.shape[0]
    z_lo = jnp.zeros((rows, QK_NOPE_DIM), w.dtype)
    z_hi = jnp.zeros((rows, HEAD_PAD - QK_NOPE_DIM - QK_ROPE_DIM), w.dtype)
    x1, x2 = w[:, :ROPE_HALF], w[:, ROPE_HALF:]
    return jnp.concatenate([z_lo, x1, x2, z_hi], -1), jnp.concatenate([z_lo, -x2, x1, z_hi], -1)


def _layer_weights(l, p):
    w_in, b_in = p["w_in"][l], p["b_in"][l]
    row = lambda v: v.reshape(1, -1)
    lw = {}
    b_row = row(b_in)
    gate = lambda t, i: t[:, OFF_POOL + i * D_MODEL:OFF_POOL + (i + 1) * D_MODEL]
    local = lambda t: jnp.concatenate([t[:, :OFF_SC_X], t[:, OFF_K_ROPE:OFF_POOL]], -1)
    gates_abd = lambda t: jnp.concatenate([gate(t, 0), gate(t, 1), gate(t, 3)], -1)
    lw["w1"], lw["b1"] = local(w_in).astype(BF16), local(b_row)
    lw["wg"], lw["bg"] = gates_abd(w_in).astype(BF16), gates_abd(b_row)
    lw["wg_c"], lw["bg_c"] = gate(w_in, 2).astype(BF16), gate(b_row, 2)
    lw["conv_dw"] = p["conv_dw"][l]
    lw["conv_ln_g"], lw["conv_ln_b"] = row(p["conv_ln_g"][l]), row(p["conv_ln_b"][l])
    lw["w_conv_out"] = p["w_conv_out"][l].astype(BF16)
    lw["sc_dw"] = p["sc_dw"][l]
    lw["w_sc_out"] = p["w_sc_out"][l].astype(BF16)
    lw["w_pool"] = p["w_pool"][l].astype(BF16)
    lw["pool_scale"] = row(p["pool_scale"][l])
    lw["w_pool_out"] = p["w_pool_out"][l].astype(BF16)
    kr, kr_swap = _rope_columns(w_in[:, OFF_KV_LAT:OFF_K_ROPE])
    bkr, bkr_swap = _rope_columns(b_in[None, OFF_KV_LAT:OFF_K_ROPE])
    lw["w2"] = jnp.concatenate([w_in[:, OFF_SC_X:OFF_KV_LAT], kr, kr_swap], -1).astype(BF16)
    lw["b2"] = jnp.concatenate([row(b_in[OFF_SC_X:OFF_KV_LAT]), bkr, bkr_swap], -1)
    lw["q_norm_g"], lw["kv_norm_g"] = row(p["q_norm_g"][l]), row(p["kv_norm_g"][l])
    w_uq = p["w_uq"][l].reshape(Q_LORA_RANK, MLA_HEADS, QK_NOPE_DIM + QK_ROPE_DIM)
    zq = jnp.zeros((Q_LORA_RANK, MLA_HEADS, HEAD_PAD - QK_NOPE_DIM - QK_ROPE_DIM), F32)
    x1 = w_uq[..., QK_NOPE_DIM:QK_NOPE_DIM + ROPE_HALF]
    x2 = w_uq[..., QK_NOPE_DIM + ROPE_HALF:]
    lw["wq"] = jnp.concatenate([w_uq, zq], -1).reshape(Q_LORA_RANK, -1).astype(BF16)
    lw["wq_swap"] = jnp.concatenate([jnp.zeros_like(w_uq[..., :QK_NOPE_DIM]), -x2, x1, zq], -1).reshape(
        Q_LORA_RANK, -1).astype(BF16)
    w_ukv = p["w_ukv"][l].reshape(KV_LORA_RANK, MLA_HEADS, QK_NOPE_DIM + V_HEAD_DIM)
    zk = jnp.zeros((KV_LORA_RANK, MLA_HEADS, HEAD_PAD - QK_NOPE_DIM), F32)
    zv = jnp.zeros((KV_LORA_RANK, MLA_HEADS, HEAD_PAD - V_HEAD_DIM), F32)
    lw["wk"] = jnp.concatenate([w_ukv[..., :QK_NOPE_DIM], zk], -1).reshape(KV_LORA_RANK, -1).astype(BF16)
    lw["wv"] = jnp.concatenate([w_ukv[..., QK_NOPE_DIM:], zv], -1).reshape(KV_LORA_RANK, -1).astype(BF16)
    lw["v_one"] = jnp.tile((jnp.arange(HEAD_PAD) == V_HEAD_DIM).astype(F32), MLA_HEADS).reshape(1, -1)
    w_mo = p["w_mla_out"][l].reshape(MLA_HEADS, V_HEAD_DIM, D_MODEL)
    lw["w_mla_out"] = jnp.concatenate(
        [w_mo, jnp.zeros((MLA_HEADS, HEAD_PAD - V_HEAD_DIM, D_MODEL), F32)], 1).astype(BF16)
    lw["w_o"] = p["w_o"][l].astype(BF16)
    lw["ln1_g"], lw["ln1_b"] = row(p["ln1_g"][l]), row(p["ln1_b"][l])
    chunked = lambda w: jnp.moveaxis(w.reshape(w.shape[0], FFN_NCHUNK, FFN_CHUNK), 1, 0)
    w_up, dw = p["w_up"][l], p["ffn_dw"][l]
    lw["w_up_v"], lw["w_up_g"] = chunked(w_up[:, :D_FF]).astype(BF16), chunked(w_up[:, D_FF:]).astype(BF16)
    lw["ffn_dw_v"], lw["ffn_dw_g"] = chunked(dw[:, :D_FF]), chunked(dw[:, D_FF:])
    lw["w_down"] = p["w_down"][l].reshape(FFN_NCHUNK, FFN_CHUNK, D_MODEL).astype(BF16)
    lw["ln2_g"], lw["ln2_b"] = row(p["ln2_g"][l]), row(p["ln2_b"][l])
    return lw


def _forward(x, c, positions, p, ts, tq):
    cos_t, sin_t = _rope_tables(positions)
    mod = _modulation(c, p["w_ada"], p["b_ada"])
    for l in range(DEPTH):
        lw = _layer_weights(l, p)
        part = _local_mixers(x, mod[l], lw, ts)
        q, k, v = _mla_proj(x, mod[l], cos_t, sin_t, lw, ts)
        attn = _flash_attention(q, k, v, tq)
        x = _merge(x, mod[l], part, attn, lw, ts)
        x = _ffn(x, mod[l], lw, ts)
    return x


def kernel(x, c, positions, w_ada, b_ada, w_in, b_in, conv_dw, conv_ln_g, conv_ln_b, w_conv_out, sc_dw, w_sc_out, q_norm_g, w_uq, kv_norm_g, w_ukv, w_mla_out, w_pool, pool_scale, w_pool_out, w_o, ln1_g, ln1_b, w_up, ffn_dw, w_down, ln2_g, ln2_b):
    p = dict(w_ada=w_ada, b_ada=b_ada, w_in=w_in, b_in=b_in, conv_dw=conv_dw, conv_ln_g=conv_ln_g,
             conv_ln_b=conv_ln_b, w_conv_out=w_conv_out, sc_dw=sc_dw, w_sc_out=w_sc_out, q_norm_g=q_norm_g,
             w_uq=w_uq, kv_norm_g=kv_norm_g, w_ukv=w_ukv, w_mla_out=w_mla_out, w_pool=w_pool,
             pool_scale=pool_scale, w_pool_out=w_pool_out, w_o=w_o, ln1_g=ln1_g, ln1_b=ln1_b, w_up=w_up,
             ffn_dw=ffn_dw, w_down=w_down, ln2_g=ln2_g, ln2_b=ln2_b)
    s = x.shape[1]
    ts = min(512, s)
    tq = min(1024, s)
    assert s % ts == 0 and s % tq == 0 and ts >= HIST_A
    return _forward(x, c, positions, p, ts, tq)
```

```python
import functools

import jax
import jax.numpy as jnp
from jax import lax
from jax.experimental import pallas as pl
from jax.experimental.pallas import tpu as pltpu

F32 = jnp.float32
BF16 = jnp.bfloat16

D_MODEL = 1024
DEPTH = 4
CONV_DIM = 512
CONV_WIDTH = 31
SC_DIM = 512
SC_WIDTH = 3
MLA_HEADS = 8
QK_NOPE_DIM = 64
QK_ROPE_DIM = 32
V_HEAD_DIM = 64
Q_LORA_RANK = 256
KV_LORA_RANK = 128
ROPE_THETA = 10000.0
POOL_WINDOWS = (2, 4, 8, 16)
POOL_GROUPS = 4
POOL_DIM = 512
POOL_GROUP_DIM = POOL_DIM // POOL_GROUPS
N_BRANCHES = 4
D_FF = 2816
FFN_CONV_WIDTH = 3
LN_EPS = 1e-5
RMS_EPS = 1e-6
DEEPNORM_ALPHA = (2.0 * DEPTH) ** 0.25

OFF_CONV_A = CONV_DIM
OFF_CONV_B = OFF_CONV_A + CONV_DIM
OFF_SC_B = OFF_CONV_B + SC_DIM
OFF_SC_C = OFF_SC_B + SC_DIM
OFF_SC_X = OFF_SC_C + SC_DIM
OFF_Q_LAT = OFF_SC_X + Q_LORA_RANK
OFF_KV_LAT = OFF_Q_LAT + KV_LORA_RANK
OFF_K_ROPE = OFF_KV_LAT + QK_ROPE_DIM
OFF_POOL = OFF_K_ROPE + POOL_DIM

LANES = 128
SUBLANES = 8
HEAD_PAD = LANES
ROPE_HALF = QK_ROPE_DIM // 2
LOG2_E = 1.4426950408889634
ATTN_SCALE = (QK_NOPE_DIM + QK_ROPE_DIM) ** -0.5 * LOG2_E
SOFTMAX_ROWS = 16
MASK_VALUE = -1e30

HIST_A = 32
HIST_B = 8
HIST_P = 16
HIST_F = 8

FFN_CHUNK = 256
FFN_NCHUNK = D_FF // FFN_CHUNK
VMEM_LIMIT = 56 * 1024 * 1024


def _dot(a, b):
    return jnp.dot(a, b, preferred_element_type=F32)


def _sigmoid(x):
    return jax.nn.sigmoid(x)


def _silu(x):
    return x * jax.nn.sigmoid(x)


def _layer_norm(x, g, b):
    mu = jnp.mean(x, axis=-1, keepdims=True)
    xc = x - mu
    var = jnp.mean(xc * xc, axis=-1, keepdims=True)
    return xc * lax.rsqrt(var + LN_EPS) * g + b


def _rms_norm(x, g):
    return x * lax.rsqrt(jnp.mean(x * x, axis=-1, keepdims=True) + RMS_EPS) * g


def _const_spec(shape):
    n = len(shape)
    return pl.BlockSpec(shape, lambda *_: (0,) * n, pipeline_mode=pl.Buffered(1))


def _params(n_grid, flags=None):
    return pltpu.CompilerParams(dimension_semantics=("arbitrary",) * n_grid, vmem_limit_bytes=VMEM_LIMIT,
                                flags=flags)


def _mod_kernel(c_ref, w_ref, b_ref, o_ref):
    c = c_ref[...]
    o_ref[...] = _dot(_silu(c).astype(BF16), w_ref[...].astype(BF16)) + b_ref[...]


def _modulation(c, w_ada, b_ada):
    n_layers, d, cols = w_ada.shape
    b = c.shape[0]
    rows = SUBLANES
    nc = 4
    cw = cols // nc
    c_pad = jnp.zeros((rows, d), F32).at[:b].set(c)
    out = pl.pallas_call(
        _mod_kernel,
        grid=(n_layers, nc),
        in_specs=[
            pl.BlockSpec((rows, d), lambda l, j: (0, 0)),
            pl.BlockSpec((None, d, cw), lambda l, j: (l, 0, j)),
            pl.BlockSpec((None, 1, cw), lambda l, j: (l, 0, j)),
        ],
        out_specs=pl.BlockSpec((None, rows, cw), lambda l, j: (l, 0, j)),
        out_shape=jax.ShapeDtypeStruct((n_layers, rows, cols), F32),
        compiler_params=_params(2),
        name="adaln_mod",
    )(c_pad, w_ada, b_ada.reshape(n_layers, 1, cols))
    return out[:, :b].reshape(n_layers, b, 6, d)


def _causal_taps(buf, taps_ref, hist, ts):
    k_taps = taps_ref.shape[0]
    acc = None
    for k in range(k_taps):
        term = taps_ref[k:k + 1, :] * buf[pl.ds(hist - (k_taps - 1) + k, ts), :]
        acc = term if acc is None else acc + term
    return acc


def _local_kernel(x_ref, mod_ref, w1_ref, b1_ref, wg_ref, bg_ref, cdw_ref, lng_ref, lnb_ref, wco_ref,
                  sdw_ref, wso_ref, wp_ref, ps_ref, wpo_ref, part_ref, buf_a, buf_b, buf_p, *, ts):
    s_idx = pl.program_id(1)

    @pl.when(s_idx == 0)
    def _():
        buf_a[0:HIST_A, :] = jnp.zeros((HIST_A, CONV_DIM), F32)
        buf_b[0:HIST_B, :] = jnp.zeros((HIST_B, SC_DIM), F32)
        buf_p[0:HIST_P, :] = jnp.zeros((HIST_P, POOL_DIM), F32)

    mod = mod_ref[...]
    h = (x_ref[...] * (1.0 + mod[1:2, :]) + mod[0:1, :]).astype(BF16)

    def proj(lo, hi):
        return _dot(h, w1_ref[:, lo:hi]) + b1_ref[:, lo:hi]

    buf_a[HIST_A:HIST_A + ts, :] = proj(0, 512) * _sigmoid(proj(512, 1024))
    ya = _causal_taps(buf_a, cdw_ref, HIST_A, ts)
    buf_a[0:HIST_A, :] = buf_a[ts:ts + HIST_A, :]
    ya = _silu(_layer_norm(ya, lng_ref[...], lnb_ref[...]))
    ya = _dot(ya.astype(BF16), wco_ref[...])
    part = _sigmoid(_dot(h, wg_ref[:, 0:D_MODEL]) + bg_ref[:, 0:D_MODEL]) * ya

    sc_b = proj(1024, 1536)
    buf_b[HIST_B:HIST_B + ts, :] = proj(1536, 2048) * proj(2048, 2560)
    yb = sc_b * _causal_taps(buf_b, sdw_ref, HIST_B, ts)
    buf_b[0:HIST_B, :] = buf_b[ts:ts + HIST_B, :]
    yb = _dot(yb.astype(BF16), wso_ref[...])
    part += _sigmoid(_dot(h, wg_ref[:, D_MODEL:2 * D_MODEL]) + bg_ref[:, D_MODEL:2 * D_MODEL]) * yb

    u = proj(2560, 3072)
    buf_p[HIST_P:HIST_P + ts, :] = u
    pos = s_idx * ts + lax.broadcasted_iota(jnp.int32, (ts, 1), 0)
    groups = []
    for g, w in enumerate(POOL_WINDOWS):
        lo, hi = g * POOL_GROUP_DIM, (g + 1) * POOL_GROUP_DIM
        tot = u[:, lo:hi]
        for j in range(1, w):
            tot = tot + buf_p[pl.ds(HIST_P - j, ts), lo:hi]
        cnt = jnp.minimum(pos + 1, w).astype(F32)
        pd = tot / cnt - u[:, lo:hi]
        groups.append(_dot(pd.astype(BF16), wp_ref[g]))
    buf_p[0:HIST_P, :] = buf_p[ts:ts + HIST_P, :]
    yd = jnp.concatenate(groups, axis=-1) * ps_ref[...]
    yd = _dot(yd.astype(BF16), wpo_ref[...])
    part += _sigmoid(_dot(h, wg_ref[:, 2 * D_MODEL:3 * D_MODEL]) + bg_ref[:, 2 * D_MODEL:3 * D_MODEL]) * yd

    part_ref[...] = part


def _local_mixers(x, mod_l, lw, ts):
    b, s, d = x.shape
    weights = [lw["w1"], lw["b1"], lw["wg"], lw["bg"], lw["conv_dw"], lw["conv_ln_g"], lw["conv_ln_b"],
               lw["w_conv_out"], lw["sc_dw"], lw["w_sc_out"], lw["w_pool"], lw["pool_scale"], lw["w_pool_out"]]
    return pl.pallas_call(
        functools.partial(_local_kernel, ts=ts),
        grid=(b, s // ts),
        in_specs=[
            pl.BlockSpec((None, ts, d), lambda i, j: (i, j, 0)),
            pl.BlockSpec((None, 6, d), lambda i, j: (i, 0, 0)),
        ] + [_const_spec(w.shape) for w in weights],
        out_specs=pl.BlockSpec((None, ts, d), lambda i, j: (i, j, 0)),
        out_shape=jax.ShapeDtypeStruct((b, s, d), F32),
        scratch_shapes=[
            pltpu.VMEM((HIST_A + ts, CONV_DIM), F32),
            pltpu.VMEM((HIST_B + ts, SC_DIM), F32),
            pltpu.VMEM((HIST_P + ts, POOL_DIM), F32),
        ],
        compiler_params=_params(2),
        name="local_mixers",
    )(x, mod_l, *weights)


def _mla_proj_kernel(x_ref, mod_ref, cos_ref, sin_ref, w2_ref, b2_ref, qg_ref, kvg_ref, wq_ref, wqs_ref,
                     wk_ref, wv_ref, vone_ref, q_ref, k_ref, v_ref):
    mod = mod_ref[...]
    h = (x_ref[...] * (1.0 + mod[1:2, :]) + mod[0:1, :]).astype(BF16)
    lat = _dot(h, w2_ref[...]) + b2_ref[...]
    o_kv = Q_LORA_RANK
    o_kr = o_kv + KV_LORA_RANK
    qn = _rms_norm(lat[:, 0:o_kv], qg_ref[...]).astype(BF16)
    kvn = _rms_norm(lat[:, o_kv:o_kr], kvg_ref[...]).astype(BF16)
    cos = cos_ref[...]
    sin = sin_ref[...]
    k_rot = lat[:, o_kr:o_kr + LANES] * cos + lat[:, o_kr + LANES:o_kr + 2 * LANES] * sin
    qa = _dot(qn, wq_ref[...])
    qb = _dot(qn, wqs_ref[...])
    kf = _dot(kvn, wk_ref[...])
    vf = _dot(kvn, wv_ref[...]) + vone_ref[...]
    for hh in range(MLA_HEADS):
        lo, hi = hh * HEAD_PAD, (hh + 1) * HEAD_PAD
        q_ref[hh] = ((qa[:, lo:hi] * cos + qb[:, lo:hi] * sin) * ATTN_SCALE).astype(BF16)
        k_ref[hh] = (kf[:, lo:hi] + k_rot).astype(BF16)
        v_ref[hh] = vf[:, lo:hi].astype(BF16)


def _mla_proj(x, mod_l, cos_t, sin_t, lw, ts):
    b, s, d = x.shape
    weights = [lw["w2"], lw["b2"], lw["q_norm_g"], lw["kv_norm_g"], lw["wq"], lw["wq_swap"], lw["wk"], lw["wv"],
               lw["v_one"]]
    head_spec = pl.BlockSpec((None, MLA_HEADS, ts, HEAD_PAD), lambda i, j: (i, 0, j, 0))
    head_shape = jax.ShapeDtypeStruct((b, MLA_HEADS, s, HEAD_PAD), BF16)
    return pl.pallas_call(
        _mla_proj_kernel,
        grid=(b, s // ts),
        in_specs=[
            pl.BlockSpec((None, ts, d), lambda i, j: (i, j, 0)),
            pl.BlockSpec((None, 6, d), lambda i, j: (i, 0, 0)),
            pl.BlockSpec((None, ts, LANES), lambda i, j: (i, j, 0)),
            pl.BlockSpec((None, ts, LANES), lambda i, j: (i, j, 0)),
        ] + [_const_spec(w.shape) for w in weights],
        out_specs=[head_spec, head_spec, head_spec],
        out_shape=[head_shape, head_shape, head_shape],
        compiler_params=_params(2),
        name="mla_proj",
    )(x, mod_l, cos_t, sin_t, *weights)


def _flash_kernel(qi_ref, ki_ref, q_ref, k_ref, v_ref, o_ref, m_ref, acc_ref, s0, s1, p0, p1, a0, a1, *, tq):
    t = pl.program_id(1)
    qi = qi_ref[t]
    ki = ki_ref[t]

    @pl.when(ki == 0)
    def _():
        m_ref[...] = jnp.full(m_ref.shape, MASK_VALUE, F32)
        acc_ref[...] = jnp.zeros(acc_ref.shape, F32)

    def scores(hh, s_ref):
        s_ref[...] = lax.dot_general(q_ref[hh], k_ref[hh], (((1,), (1,)), ((), ())), preferred_element_type=F32)

    def softmax(hh, s_ref, p_ref, a_ref, masked):
        for r in range(tq // SOFTMAX_ROWS):
            rows = pl.ds(r * SOFTMAX_ROWS, SOFTMAX_ROWS)
            s = s_ref[rows, :]
            if masked:
                row = r * SOFTMAX_ROWS + lax.broadcasted_iota(jnp.int32, s.shape, 0)
                col = lax.broadcasted_iota(jnp.int32, s.shape, 1)
                s = jnp.where(col <= row, s, MASK_VALUE)
            m_old = m_ref[hh, rows, :]
            m_new = jnp.maximum(m_old, jnp.max(s, axis=1, keepdims=True))
            a_ref[rows, :] = jnp.exp2(m_old - m_new)
            p_ref[rows, :] = jnp.exp2(s - m_new[:, 0:1]).astype(BF16)
            m_ref[hh, rows, :] = m_new

    def update(hh, p_ref, a_ref):
        acc_ref[hh] = a_ref[...] * acc_ref[hh] + _dot(p_ref[...], v_ref[hh])

    def block(masked):
        scores(0, s0)
        scores(1, s1)
        softmax(0, s0, p0, a0, masked)

        def pair(j, carry):
            ha = 2 * j + 1
            hb = ha + 1
            scores(hb, s0)
            softmax(ha, s1, p1, a1, masked)
            update(ha - 1, p0, a0)
            scores(hb + 1, s1)
            softmax(hb, s0, p0, a0, masked)
            update(ha, p1, a1)
            return carry
        lax.fori_loop(0, (MLA_HEADS - 2) // 2, pair, 0)
        softmax(MLA_HEADS - 1, s1, p1, a1, masked)
        update(MLA_HEADS - 2, p0, a0)
        update(MLA_HEADS - 1, p1, a1)

    @pl.when(ki < qi)
    def _():
        block(False)

    @pl.when(ki == qi)
    def _():
        block(True)

        def finish(hh, carry):
            acc = acc_ref[hh]
            o_ref[hh] = (acc / acc[:, V_HEAD_DIM:V_HEAD_DIM + 1]).astype(BF16)
            return carry
        lax.fori_loop(0, MLA_HEADS, finish, 0)


def _flash_attention(q, k, v, tq):
    b, nh, s, hp = q.shape
    nq = s // tq
    pairs = [(i, j) for i in range(nq) for j in range(i + 1)]
    qi_tab = jnp.asarray([p[0] for p in pairs], jnp.int32)
    ki_tab = jnp.asarray([p[1] for p in pairs], jnp.int32)
    q_spec = pl.BlockSpec((None, nh, tq, hp), lambda i, t, qi, ki: (i, 0, qi[t], 0))
    kv_spec = pl.BlockSpec((None, nh, tq, hp), lambda i, t, qi, ki: (i, 0, ki[t], 0))
    return pl.pallas_call(
        functools.partial(_flash_kernel, tq=tq),
        grid_spec=pltpu.PrefetchScalarGridSpec(
            num_scalar_prefetch=2,
            grid=(b, len(pairs)),
            in_specs=[q_spec, kv_spec, kv_spec],
            out_specs=q_spec,
            scratch_shapes=[
                pltpu.VMEM((nh, tq, LANES), F32),
                pltpu.VMEM((nh, tq, hp), F32),
                pltpu.VMEM((tq, tq), F32),
                pltpu.VMEM((tq, tq), F32),
                pltpu.VMEM((tq, tq), BF16),
                pltpu.VMEM((tq, tq), BF16),
                pltpu.VMEM((tq, LANES), F32),
                pltpu.VMEM((tq, LANES), F32),
            ],
        ),
        out_shape=jax.ShapeDtypeStruct((b, nh, s, hp), BF16),
        compiler_params=_params(2),
        name="flash_attention",
    )(qi_tab, ki_tab, q, k, v)


def _merge_kernel(x_ref, mod_ref, part_ref, o_ref, wg_ref, bg_ref, wmo_ref, wo_ref, g_ref, b_ref, out_ref):
    mod = mod_ref[...]
    x = x_ref[...]
    h = (x * (1.0 + mod[1:2, :]) + mod[0:1, :]).astype(BF16)
    gate = _sigmoid(_dot(h, wg_ref[...]) + bg_ref[...])
    yc = None
    for hh in range(MLA_HEADS):
        term = _dot(o_ref[hh], wmo_ref[hh])
        yc = term if yc is None else yc + term
    merged = part_ref[...] + gate * yc
    mix = _dot(merged.astype(BF16), wo_ref[...])
    out_ref[...] = _layer_norm(DEEPNORM_ALPHA * x + (1.0 + mod[2:3, :]) * mix, g_ref[...], b_ref[...])


def _merge(x, mod_l, part, attn, lw, ts):
    b, s, d = x.shape
    weights = [lw["wg_c"], lw["bg_c"], lw["w_mla_out"], lw["w_o"], lw["ln1_g"], lw["ln1_b"]]
    tile = pl.BlockSpec((None, ts, d), lambda i, j: (i, j, 0))
    return pl.pallas_call(
        _merge_kernel,
        grid=(b, s // ts),
        in_specs=[
            tile,
            pl.BlockSpec((None, 6, d), lambda i, j: (i, 0, 0)),
            tile,
            pl.BlockSpec((None, MLA_HEADS, ts, HEAD_PAD), lambda i, j: (i, 0, j, 0)),
        ] + [_const_spec(w.shape) for w in weights],
        out_specs=tile,
        out_shape=jax.ShapeDtypeStruct((b, s, d), F32),
        compiler_params=_params(2),
        name="merge_ln",
    )(x, mod_l, part, attn, *weights)


def _ffn_kernel(x_ref, mod_ref, wv_ref, wgt_ref, dwv_ref, dwg_ref, wd_ref, g_ref, b_ref, out_ref,
                hist_v, hist_g, buf_v, buf_g, acc_ref, *, ts):
    s_idx = pl.program_id(1)

    @pl.when(s_idx == 0)
    def _():
        hist_v[...] = jnp.zeros(hist_v.shape, F32)
        hist_g[...] = jnp.zeros(hist_g.shape, F32)

    mod = mod_ref[...]
    x = x_ref[...]
    h = (x * (1.0 + mod[4:5, :]) + mod[3:4, :]).astype(BF16)
    acc_ref[...] = jnp.zeros(acc_ref.shape, F32)

    def conv(w_ref, dw_ref, hist, buf, c):
        buf[0:HIST_F, :] = hist[c]
        buf[HIST_F:HIST_F + ts, :] = _dot(h, w_ref[c])
        hist[c] = buf[ts:ts + HIST_F, :]
        taps = dw_ref[c]
        out = None
        for k in range(FFN_CONV_WIDTH):
            term = taps[k:k + 1, :] * buf[pl.ds(HIST_F - (FFN_CONV_WIDTH - 1) + k, ts), :]
            out = term if out is None else out + term
        return out

    def chunk(c, carry):
        val = conv(wv_ref, dwv_ref, hist_v, buf_v, c)
        gate = conv(wgt_ref, dwg_ref, hist_g, buf_g, c)
        acc_ref[...] += _dot((_silu(gate) * val).astype(BF16), wd_ref[c])
        return carry

    lax.fori_loop(0, FFN_NCHUNK, chunk, 0)
    out_ref[...] = _layer_norm(DEEPNORM_ALPHA * x + (1.0 + mod[5:6, :]) * acc_ref[...], g_ref[...], b_ref[...])


def _ffn(x, mod_l, lw, ts):
    b, s, d = x.shape
    weights = [lw["w_up_v"], lw["w_up_g"], lw["ffn_dw_v"], lw["ffn_dw_g"], lw["w_down"], lw["ln2_g"], lw["ln2_b"]]
    tile = pl.BlockSpec((None, ts, d), lambda i, j: (i, j, 0))
    return pl.pallas_call(
        functools.partial(_ffn_kernel, ts=ts),
        grid=(b, s // ts),
        in_specs=[tile, pl.BlockSpec((None, 6, d), lambda i, j: (i, 0, 0))]
        + [_const_spec(w.shape) for w in weights],
        out_specs=tile,
        out_shape=jax.ShapeDtypeStruct((b, s, d), F32),
        scratch_shapes=[
            pltpu.VMEM((FFN_NCHUNK, HIST_F, FFN_CHUNK), F32),
            pltpu.VMEM((FFN_NCHUNK, HIST_F, FFN_CHUNK), F32),
            pltpu.VMEM((HIST_F + ts, FFN_CHUNK), F32),
            pltpu.VMEM((HIST_F + ts, FFN_CHUNK), F32),
            pltpu.VMEM((ts, d), F32),
        ],
        compiler_params=_params(2),
        name="conv_glu_ffn",
    )(x, mod_l, *weights)


def _rope_tables(positions):
    inv = 1.0 / (ROPE_THETA ** (jnp.arange(0, QK_ROPE_DIM, 2, dtype=F32) / QK_ROPE_DIM))
    ang = positions.astype(F32)[..., None] * inv
    cos, sin = jnp.cos(ang), jnp.sin(ang)
    lead = cos.shape[:-1]
    pad = HEAD_PAD - QK_NOPE_DIM - QK_ROPE_DIM
    cos_t = jnp.concatenate([jnp.ones(lead + (QK_NOPE_DIM,), F32), cos, cos, jnp.ones(lead + (pad,), F32)], -1)
    sin_t = jnp.concatenate([jnp.zeros(lead + (QK_NOPE_DIM,), F32), sin, sin, jnp.zeros(lead + (pad,), F32)], -1)
    return cos_t, sin_t


def _rope_columns(w):
    rows = w.shape[0]
    z_lo = jnp.zeros((rows, QK_NOPE_DIM), w.dtype)
    z_hi = jnp.zeros((rows, HEAD_PAD - QK_NOPE_DIM - QK_ROPE_DIM), w.dtype)
    x1, x2 = w[:, :ROPE_HALF], w[:, ROPE_HALF:]
    return jnp.concatenate([z_lo, x1, x2, z_hi], -1), jnp.concatenate([z_lo, -x2, x1, z_hi], -1)


def _layer_weights(l, p):
    w_in, b_in = p["w_in"][l], p["b_in"][l]
    row = lambda v: v.reshape(1, -1)
    lw = {}
    b_row = row(b_in)
    gate = lambda t, i: t[:, OFF_POOL + i * D_MODEL:OFF_POOL + (i + 1) * D_MODEL]
    local = lambda t: jnp.concatenate([t[:, :OFF_SC_X], t[:, OFF_K_ROPE:OFF_POOL]], -1)
    gates_abd = lambda t: jnp.concatenate([gate(t, 0), gate(t, 1), gate(t, 3)], -1)
    lw["w1"], lw["b1"] = local(w_in).astype(BF16), local(b_row)
    lw["wg"], lw["bg"] = gates_abd(w_in).astype(BF16), gates_abd(b_row)
    lw["wg_c"], lw["bg_c"] = gate(w_in, 2).astype(BF16), gate(b_row, 2)
    lw["conv_dw"] = p["conv_dw"][l]
    lw["conv_ln_g"], lw["conv_ln_b"] = row(p["conv_ln_g"][l]), row(p["conv_ln_b"][l])
    lw["w_conv_out"] = p["w_conv_out"][l].astype(BF16)
    lw["sc_dw"] = p["sc_dw"][l]
    lw["w_sc_out"] = p["w_sc_out"][l].astype(BF16)
    lw["w_pool"] = p["w_pool"][l].astype(BF16)
    lw["pool_scale"] = row(p["pool_scale"][l])
    lw["w_pool_out"] = p["w_pool_out"][l].astype(BF16)
    kr, kr_swap = _rope_columns(w_in[:, OFF_KV_LAT:OFF_K_ROPE])
    bkr, bkr_swap = _rope_columns(b_in[None, OFF_KV_LAT:OFF_K_ROPE])
    lw["w2"] = jnp.concatenate([w_in[:, OFF_SC_X:OFF_KV_LAT], kr, kr_swap], -1).astype(BF16)
    lw["b2"] = jnp.concatenate([row(b_in[OFF_SC_X:OFF_KV_LAT]), bkr, bkr_swap], -1)
    lw["q_norm_g"], lw["kv_norm_g"] = row(p["q_norm_g"][l]), row(p["kv_norm_g"][l])
    w_uq = p["w_uq"][l].reshape(Q_LORA_RANK, MLA_HEADS, QK_NOPE_DIM + QK_ROPE_DIM)
    zq = jnp.zeros((Q_LORA_RANK, MLA_HEADS, HEAD_PAD - QK_NOPE_DIM - QK_ROPE_DIM), F32)
    x1 = w_uq[..., QK_NOPE_DIM:QK_NOPE_DIM + ROPE_HALF]
    x2 = w_uq[..., QK_NOPE_DIM + ROPE_HALF:]
    lw["wq"] = jnp.concatenate([w_uq, zq], -1).reshape(Q_LORA_RANK, -1).astype(BF16)
    lw["wq_swap"] = jnp.concatenate([jnp.zeros_like(w_uq[..., :QK_NOPE_DIM]), -x2, x1, zq], -1).reshape(
        Q_LORA_RANK, -1).astype(BF16)
    w_ukv = p["w_ukv"][l].reshape(KV_LORA_RANK, MLA_HEADS, QK_NOPE_DIM + V_HEAD_DIM)
    zk = jnp.zeros((KV_LORA_RANK, MLA_HEADS, HEAD_PAD - QK_NOPE_DIM), F32)
    zv = jnp.zeros((KV_LORA_RANK, MLA_HEADS, HEAD_PAD - V_HEAD_DIM), F32)
    lw["wk"] = jnp.concatenate([w_ukv[..., :QK_NOPE_DIM], zk], -1).reshape(KV_LORA_RANK, -1).astype(BF16)
    lw["wv"] = jnp.concatenate([w_ukv[..., QK_NOPE_DIM:], zv], -1).reshape(KV_LORA_RANK, -1).astype(BF16)
    lw["v_one"] = jnp.tile((jnp.arange(HEAD_PAD) == V_HEAD_DIM).astype(F32), MLA_HEADS).reshape(1, -1)
    w_mo = p["w_mla_out"][l].reshape(MLA_HEADS, V_HEAD_DIM, D_MODEL)
    lw["w_mla_out"] = jnp.concatenate(
        [w_mo, jnp.zeros((MLA_HEADS, HEAD_PAD - V_HEAD_DIM, D_MODEL), F32)], 1).astype(BF16)
    lw["w_o"] = p["w_o"][l].astype(BF16)
    lw["ln1_g"], lw["ln1_b"] = row(p["ln1_g"][l]), row(p["ln1_b"][l])
    chunked = lambda w: jnp.moveaxis(w.reshape(w.shape[0], FFN_NCHUNK, FFN_CHUNK), 1, 0)
    w_up, dw = p["w_up"][l], p["ffn_dw"][l]
    lw["w_up_v"], lw["w_up_g"] = chunked(w_up[:, :D_FF]).astype(BF16), chunked(w_up[:, D_FF:]).astype(BF16)
    lw["ffn_dw_v"], lw["ffn_dw_g"] = chunked(dw[:, :D_FF]), chunked(dw[:, D_FF:])
    lw["w_down"] = p["w_down"][l].reshape(FFN_NCHUNK, FFN_CHUNK, D_MODEL).astype(BF16)
    lw["ln2_g"], lw["ln2_b"] = row(p["ln2_g"][l]), row(p["ln2_b"][l])
    return lw


def _tiles(s):
    ts = min(512, s)
    tq = min(1024, s)
    assert s % ts == 0 and s % tq == 0 and ts >= HIST_A and tq % SOFTMAX_ROWS == 0
    return ts, tq


def _forward(x, c, positions, p, ts, tq):
    cos_t, sin_t = _rope_tables(positions)
    mod = _modulation(c, p["w_ada"], p["b_ada"])
    for l in range(DEPTH):
        lw = _layer_weights(l, p)
        part = _local_mixers(x, mod[l], lw, ts)
        q, k, v = _mla_proj(x, mod[l], cos_t, sin_t, lw, ts)
        attn = _flash_attention(q, k, v, tq)
        x = _merge(x, mod[l], part, attn, lw, ts)
        x = _ffn(x, mod[l], lw, ts)
    return x


def kernel(x, c, positions, w_ada, b_ada, w_in, b_in, conv_dw, conv_ln_g, conv_ln_b, w_conv_out, sc_dw, w_sc_out, q_norm_g, w_uq, kv_norm_g, w_ukv, w_mla_out, w_pool, pool_scale, w_pool_out, w_o, ln1_g, ln1_b, w_up, ffn_dw, w_down, ln2_g, ln2_b):
    p = dict(w_ada=w_ada, b_ada=b_ada, w_in=w_in, b_in=b_in, conv_dw=conv_dw, conv_ln_g=conv_ln_g,
             conv_ln_b=conv_ln_b, w_conv_out=w_conv_out, sc_dw=sc_dw, w_sc_out=w_sc_out, q_norm_g=q_norm_g,
             w_uq=w_uq, kv_norm_g=kv_norm_g, w_ukv=w_ukv, w_mla_out=w_mla_out, w_pool=w_pool,
             pool_scale=pool_scale, w_pool_out=w_pool_out, w_o=w_o, ln1_g=ln1_g, ln1_b=ln1_b, w_up=w_up,
             ffn_dw=ffn_dw, w_down=w_down, ln2_g=ln2_g, ln2_b=ln2_b)
    ts, tq = _tiles(x.shape[1])
    return _forward(x, c, positions, p, ts, tq)
```

```python
import functools

import jax
import jax.numpy as jnp
from jax import lax
from jax.experimental import pallas as pl
from jax.experimental.pallas import tpu as pltpu

F32 = jnp.float32
BF16 = jnp.bfloat16

D_MODEL = 1024
DEPTH = 4
CONV_DIM = 512
CONV_WIDTH = 31
SC_DIM = 512
SC_WIDTH = 3
MLA_HEADS = 8
QK_NOPE_DIM = 64
QK_ROPE_DIM = 32
V_HEAD_DIM = 64
Q_LORA_RANK = 256
KV_LORA_RANK = 128
ROPE_THETA = 10000.0
POOL_WINDOWS = (2, 4, 8, 16)
POOL_GROUPS = 4
POOL_DIM = 512
POOL_GROUP_DIM = POOL_DIM // POOL_GROUPS
N_BRANCHES = 4
D_FF = 2816
FFN_CONV_WIDTH = 3
LN_EPS = 1e-5
RMS_EPS = 1e-6
DEEPNORM_ALPHA = (2.0 * DEPTH) ** 0.25

OFF_CONV_A = CONV_DIM
OFF_CONV_B = OFF_CONV_A + CONV_DIM
OFF_SC_B = OFF_CONV_B + SC_DIM
OFF_SC_C = OFF_SC_B + SC_DIM
OFF_SC_X = OFF_SC_C + SC_DIM
OFF_Q_LAT = OFF_SC_X + Q_LORA_RANK
OFF_KV_LAT = OFF_Q_LAT + KV_LORA_RANK
OFF_K_ROPE = OFF_KV_LAT + QK_ROPE_DIM
OFF_POOL = OFF_K_ROPE + POOL_DIM

LANES = 128
SUBLANES = 8
HEAD_PAD = LANES
ROPE_HALF = QK_ROPE_DIM // 2
LOG2_E = 1.4426950408889634
ATTN_SCALE = (QK_NOPE_DIM + QK_ROPE_DIM) ** -0.5 * LOG2_E
SOFTMAX_ROWS = 16
MASK_VALUE = -1e30

HIST_A = 32
HIST_B = 8
HIST_P = 16
HIST_F = 8

FFN_CHUNK = 256
FFN_NCHUNK = D_FF // FFN_CHUNK
assert FFN_NCHUNK % 2 == 1
VMEM_LIMIT = 56 * 1024 * 1024


def _dot(a, b):
    return jnp.dot(a, b, preferred_element_type=F32)


def _sigmoid(x):
    return jax.nn.sigmoid(x)


def _silu(x):
    return x * jax.nn.sigmoid(x)


def _layer_norm(x, g, b):
    mu = jnp.mean(x, axis=-1, keepdims=True)
    xc = x - mu
    var = jnp.mean(xc * xc, axis=-1, keepdims=True)
    return xc * lax.rsqrt(var + LN_EPS) * g + b


def _rms_norm(x, g):
    return x * lax.rsqrt(jnp.mean(x * x, axis=-1, keepdims=True) + RMS_EPS) * g


def _const_spec(shape):
    n = len(shape)
    return pl.BlockSpec(shape, lambda *_: (0,) * n, pipeline_mode=pl.Buffered(1))


def _params(n_grid, flags=None):
    return pltpu.CompilerParams(dimension_semantics=("arbitrary",) * n_grid, vmem_limit_bytes=VMEM_LIMIT,
                                flags=flags)


def _mod_kernel(c_ref, w_ref, b_ref, o_ref):
    c = c_ref[...]
    o_ref[...] = _dot(_silu(c).astype(BF16), w_ref[...].astype(BF16)) + b_ref[...]


def _modulation(c, w_ada, b_ada):
    n_layers, d, cols = w_ada.shape
    b = c.shape[0]
    rows = SUBLANES
    nc = 4
    cw = cols // nc
    c_pad = jnp.zeros((rows, d), F32).at[:b].set(c)
    out = pl.pallas_call(
        _mod_kernel,
        grid=(n_layers, nc),
        in_specs=[
            pl.BlockSpec((rows, d), lambda l, j: (0, 0)),
            pl.BlockSpec((None, d, cw), lambda l, j: (l, 0, j)),
            pl.BlockSpec((None, 1, cw), lambda l, j: (l, 0, j)),
        ],
        out_specs=pl.BlockSpec((None, rows, cw), lambda l, j: (l, 0, j)),
        out_shape=jax.ShapeDtypeStruct((n_layers, rows, cols), F32),
        compiler_params=_params(2),
        name="adaln_mod",
    )(c_pad, w_ada, b_ada.reshape(n_layers, 1, cols))
    return out[:, :b].reshape(n_layers, b, 6, d)


def _causal_taps(buf, taps_ref, hist, ts):
    k_taps = taps_ref.shape[0]
    acc = None
    for k in range(k_taps):
        term = taps_ref[k:k + 1, :] * buf[pl.ds(hist - (k_taps - 1) + k, ts), :]
        acc = term if acc is None else acc + term
    return acc


def _grouped_taps_piece(buf, taps_ref, z_ref, hist, ts, r, y):
    k_taps = taps_ref.shape[0]
    assert hist >= SUBLANES * (1 + (k_taps - 1) // SUBLANES)
    z = None
    for a in range((k_taps - 1 - r) // SUBLANES + 1):
        d = SUBLANES * a + r
        rows = pl.ds(hist - SUBLANES * (a + 1), ts + SUBLANES)
        term = taps_ref[k_taps - 1 - d:k_taps - d, :] * buf[rows, :]
        z = term if z is None else z + term
    if r == 0:
        return z[SUBLANES:, :]
    z_ref[r % 2] = z
    return y + z_ref[r % 2, pl.ds(SUBLANES - r, ts), :]


def _local_kernel(x_ref, mod_ref, w1_ref, b1_ref, wg_ref, bg_ref, cdw_ref, lng_ref, lnb_ref, wco_ref,
                  sdw_ref, wso_ref, wp_ref, ps_ref, wpo_ref, part_ref,
                  buf_a, buf_b, buf_p, buf_z, h_ref, scb_ref, gate_ref, *, ts):
    s_idx = pl.program_id(1)
    half = D_MODEL // 2

    @pl.when(s_idx == 0)
    def _():
        buf_a[0:HIST_A, :] = jnp.zeros((HIST_A, CONV_DIM), F32)
        buf_b[0:HIST_B, :] = jnp.zeros((HIST_B, SC_DIM), F32)
        buf_p[0:HIST_P, :] = jnp.zeros((HIST_P, POOL_DIM), F32)

    mod = mod_ref[...]
    h_ref[...] = (x_ref[...] * (1.0 + mod[1:2, :]) + mod[0:1, :]).astype(BF16)

    def proj(lo, hi):
        return _dot(h_ref[...], w1_ref[:, lo:hi]) + b1_ref[:, lo:hi]

    def gate_pre(i, j):
        lo = i * D_MODEL + j * half
        gate_ref[i, :, j * half:(j + 1) * half] = _dot(h_ref[...], wg_ref[:, lo:lo + half]) + bg_ref[:, lo:lo + half]

    def proj_b_gate():
        scb_ref[...] = proj(1024, 1536)

    def proj_b_conv():
        buf_b[HIST_B:HIST_B + ts, :] = proj(1536, 2048) * proj(2048, 2560)

    def proj_pool():
        buf_p[HIST_P:HIST_P + ts, :] = proj(2560, 3072)

    mxu_tasks = [proj_b_conv, proj_b_gate, proj_pool] + [functools.partial(gate_pre, i, j)
                                                        for i in range(N_BRANCHES - 1) for j in range(2)]

    buf_a[HIST_A:HIST_A + ts, :] = proj(0, 512) * _sigmoid(proj(512, 1024))
    ya = None
    for r in range(SUBLANES):
        ya = _grouped_taps_piece(buf_a, cdw_ref, buf_z, HIST_A, ts, r, ya)
        for task in mxu_tasks[r::SUBLANES]:
            task()
    buf_a[0:HIST_A, :] = buf_a[ts:ts + HIST_A, :]
    ya = _silu(_layer_norm(ya, lng_ref[...], lnb_ref[...]))
    ya = _dot(ya.astype(BF16), wco_ref[...])

    yb = scb_ref[...] * _causal_taps(buf_b, sdw_ref, HIST_B, ts)
    buf_b[0:HIST_B, :] = buf_b[ts:ts + HIST_B, :]
    yb = _dot(yb.astype(BF16), wso_ref[...])
    part = _sigmoid(gate_ref[0]) * ya

    pos = s_idx * ts + lax.broadcasted_iota(jnp.int32, (ts, 1), 0)
    groups = []
    for g, w in enumerate(POOL_WINDOWS):
        lo, hi = g * POOL_GROUP_DIM, (g + 1) * POOL_GROUP_DIM
        u = buf_p[HIST_P:HIST_P + ts, lo:hi]
        tot = u
        for j in range(1, w):
            tot = tot + buf_p[pl.ds(HIST_P - j, ts), lo:hi]
        cnt = jnp.minimum(pos + 1, w).astype(F32)
        pd = tot / cnt - u
        groups.append(_dot(pd.astype(BF16), wp_ref[g]))
    buf_p[0:HIST_P, :] = buf_p[ts:ts + HIST_P, :]
    part += _sigmoid(gate_ref[1]) * yb
    yd = jnp.concatenate(groups, axis=-1) * ps_ref[...]
    yd = _dot(yd.astype(BF16), wpo_ref[...])
    part_ref[...] = part + _sigmoid(gate_ref[2]) * yd


def _local_mixers(x, mod_l, lw, ts):
    b, s, d = x.shape
    weights = [lw["w1"], lw["b1"], lw["wg"], lw["bg"], lw["conv_dw"], lw["conv_ln_g"], lw["conv_ln_b"],
               lw["w_conv_out"], lw["sc_dw"], lw["w_sc_out"], lw["w_pool"], lw["pool_scale"], lw["w_pool_out"]]
    return pl.pallas_call(
        functools.partial(_local_kernel, ts=ts),
        grid=(b, s // ts),
        in_specs=[
            pl.BlockSpec((None, ts, d), lambda i, j: (i, j, 0)),
            pl.BlockSpec((None, 6, d), lambda i, j: (i, 0, 0)),
        ] + [_const_spec(w.shape) for w in weights],
        out_specs=pl.BlockSpec((None, ts, d), lambda i, j: (i, j, 0)),
        out_shape=jax.ShapeDtypeStruct((b, s, d), F32),
        scratch_shapes=[
            pltpu.VMEM((HIST_A + ts, CONV_DIM), F32),
            pltpu.VMEM((HIST_B + ts, SC_DIM), F32),
            pltpu.VMEM((HIST_P + ts, POOL_DIM), F32),
            pltpu.VMEM((2, SUBLANES + ts, CONV_DIM), F32),
            pltpu.VMEM((ts, d), BF16),
            pltpu.VMEM((ts, SC_DIM), F32),
            pltpu.VMEM((N_BRANCHES - 1, ts, d), F32),
        ],
        compiler_params=_params(2),
        name="local_mixers",
    )(x, mod_l, *weights)


def _mla_proj_kernel(x_ref, mod_ref, cos_ref, sin_ref, w2_ref, b2_ref, qg_ref, kvg_ref, wq_ref, wqs_ref,
                     wk_ref, wv_ref, vone_ref, q_ref, k_ref, v_ref):
    mod = mod_ref[...]
    h = (x_ref[...] * (1.0 + mod[1:2, :]) + mod[0:1, :]).astype(BF16)
    lat = _dot(h, w2_ref[...]) + b2_ref[...]
    o_kv = Q_LORA_RANK
    o_kr = o_kv + KV_LORA_RANK
    qn = _rms_norm(lat[:, 0:o_kv], qg_ref[...]).astype(BF16)
    kvn = _rms_norm(lat[:, o_kv:o_kr], kvg_ref[...]).astype(BF16)
    cos = cos_ref[...]
    sin = sin_ref[...]
    k_rot = lat[:, o_kr:o_kr + LANES] * cos + lat[:, o_kr + LANES:o_kr + 2 * LANES] * sin
    qa = _dot(qn, wq_ref[...])
    qb = _dot(qn, wqs_ref[...])
    kf = _dot(kvn, wk_ref[...])
    vf = _dot(kvn, wv_ref[...]) + vone_ref[...]
    for hh in range(MLA_HEADS):
        lo, hi = hh * HEAD_PAD, (hh + 1) * HEAD_PAD
        q_ref[hh] = ((qa[:, lo:hi] * cos + qb[:, lo:hi] * sin) * ATTN_SCALE).astype(BF16)
        k_ref[hh] = (kf[:, lo:hi] + k_rot).astype(BF16)
        v_ref[hh] = vf[:, lo:hi].astype(BF16)


def _mla_proj(x, mod_l, cos_t, sin_t, lw, ts):
    b, s, d = x.shape
    weights = [lw["w2"], lw["b2"], lw["q_norm_g"], lw["kv_norm_g"], lw["wq"], lw["wq_swap"], lw["wk"], lw["wv"],
               lw["v_one"]]
    head_spec = pl.BlockSpec((None, MLA_HEADS, ts, HEAD_PAD), lambda i, j: (i, 0, j, 0))
    head_shape = jax.ShapeDtypeStruct((b, MLA_HEADS, s, HEAD_PAD), BF16)
    return pl.pallas_call(
        _mla_proj_kernel,
        grid=(b, s // ts),
        in_specs=[
            pl.BlockSpec((None, ts, d), lambda i, j: (i, j, 0)),
            pl.BlockSpec((None, 6, d), lambda i, j: (i, 0, 0)),
            pl.BlockSpec((None, ts, LANES), lambda i, j: (i, j, 0)),
            pl.BlockSpec((None, ts, LANES), lambda i, j: (i, j, 0)),
        ] + [_const_spec(w.shape) for w in weights],
        out_specs=[head_spec, head_spec, head_spec],
        out_shape=[head_shape, head_shape, head_shape],
        compiler_params=_params(2),
        name="mla_proj",
    )(x, mod_l, cos_t, sin_t, *weights)


def _flash_kernel(qi_ref, ki_ref, q_ref, k_ref, v_ref, o_ref, m_ref, acc_ref, s0, s1, p0, p1, a0, a1, *, tq):
    t = pl.program_id(1)
    qi = qi_ref[t]
    ki = ki_ref[t]

    @pl.when(ki == 0)
    def _():
        m_ref[...] = jnp.full(m_ref.shape, MASK_VALUE, F32)
        acc_ref[...] = jnp.zeros(acc_ref.shape, F32)

    def scores(hh, s_ref):
        s_ref[...] = lax.dot_general(q_ref[hh], k_ref[hh], (((1,), (1,)), ((), ())), preferred_element_type=F32)

    def softmax(hh, s_ref, p_ref, a_ref, masked):
        for r in range(tq // SOFTMAX_ROWS):
            rows = pl.ds(r * SOFTMAX_ROWS, SOFTMAX_ROWS)
            s = s_ref[rows, :]
            if masked:
                row = r * SOFTMAX_ROWS + lax.broadcasted_iota(jnp.int32, s.shape, 0)
                col = lax.broadcasted_iota(jnp.int32, s.shape, 1)
                s = jnp.where(col <= row, s, MASK_VALUE)
            m_old = m_ref[hh, rows, :]
            m_new = jnp.maximum(m_old, jnp.max(s, axis=1, keepdims=True))
            a_ref[rows, :] = jnp.exp2(m_old - m_new)
            p_ref[rows, :] = jnp.exp2(s - m_new[:, 0:1]).astype(BF16)
            m_ref[hh, rows, :] = m_new

    def update(hh, p_ref, a_ref):
        acc_ref[hh] = a_ref[...] * acc_ref[hh] + _dot(p_ref[...], v_ref[hh])

    def block(masked):
        scores(0, s0)
        scores(1, s1)
        softmax(0, s0, p0, a0, masked)

        def pair(j, carry):
            ha = 2 * j + 1
            hb = ha + 1
            scores(hb, s0)
            softmax(ha, s1, p1, a1, masked)
            update(ha - 1, p0, a0)
            scores(hb + 1, s1)
            softmax(hb, s0, p0, a0, masked)
            update(ha, p1, a1)
            return carry
        lax.fori_loop(0, (MLA_HEADS - 2) // 2, pair, 0)
        softmax(MLA_HEADS - 1, s1, p1, a1, masked)
        update(MLA_HEADS - 2, p0, a0)
        update(MLA_HEADS - 1, p1, a1)

    @pl.when(ki < qi)
    def _():
        block(False)

    @pl.when(ki == qi)
    def _():
        block(True)

        def finish(hh, carry):
            acc = acc_ref[hh]
            o_ref[hh] = (acc / acc[:, V_HEAD_DIM:V_HEAD_DIM + 1]).astype(BF16)
            return carry
        lax.fori_loop(0, MLA_HEADS, finish, 0)


def _flash_attention(q, k, v, tq):
    b, nh, s, hp = q.shape
    nq = s // tq
    pairs = [(i, j) for i in range(nq) for j in range(i + 1)]
    qi_tab = jnp.asarray([p[0] for p in pairs], jnp.int32)
    ki_tab = jnp.asarray([p[1] for p in pairs], jnp.int32)
    q_spec = pl.BlockSpec((None, nh, tq, hp), lambda i, t, qi, ki: (i, 0, qi[t], 0))
    kv_spec = pl.BlockSpec((None, nh, tq, hp), lambda i, t, qi, ki: (i, 0, ki[t], 0))
    return pl.pallas_call(
        functools.partial(_flash_kernel, tq=tq),
        grid_spec=pltpu.PrefetchScalarGridSpec(
            num_scalar_prefetch=2,
            grid=(b, len(pairs)),
            in_specs=[q_spec, kv_spec, kv_spec],
            out_specs=q_spec,
            scratch_shapes=[
                pltpu.VMEM((nh, tq, LANES), F32),
                pltpu.VMEM((nh, tq, hp), F32),
                pltpu.VMEM((tq, tq), F32),
                pltpu.VMEM((tq, tq), F32),
                pltpu.VMEM((tq, tq), BF16),
                pltpu.VMEM((tq, tq), BF16),
                pltpu.VMEM((tq, LANES), F32),
                pltpu.VMEM((tq, LANES), F32),
            ],
        ),
        out_shape=jax.ShapeDtypeStruct((b, nh, s, hp), BF16),
        compiler_params=_params(2),
        name="flash_attention",
    )(qi_tab, ki_tab, q, k, v)


def _merge_kernel(x_ref, mod_ref, part_ref, o_ref, wg_ref, bg_ref, wmo_ref, wo_ref, g_ref, b_ref, out_ref):
    mod = mod_ref[...]
    x = x_ref[...]
    h = (x * (1.0 + mod[1:2, :]) + mod[0:1, :]).astype(BF16)
    gate = _sigmoid(_dot(h, wg_ref[...]) + bg_ref[...])
    yc = None
    for hh in range(MLA_HEADS):
        term = _dot(o_ref[hh], wmo_ref[hh])
        yc = term if yc is None else yc + term
    merged = part_ref[...] + gate * yc
    mix = _dot(merged.astype(BF16), wo_ref[...])
    out_ref[...] = _layer_norm(DEEPNORM_ALPHA * x + (1.0 + mod[2:3, :]) * mix, g_ref[...], b_ref[...])


def _merge(x, mod_l, part, attn, lw, ts):
    b, s, d = x.shape
    weights = [lw["wg_c"], lw["bg_c"], lw["w_mla_out"], lw["w_o"], lw["ln1_g"], lw["ln1_b"]]
    tile = pl.BlockSpec((None, ts, d), lambda i, j: (i, j, 0))
    return pl.pallas_call(
        _merge_kernel,
        grid=(b, s // ts),
        in_specs=[
            tile,
            pl.BlockSpec((None, 6, d), lambda i, j: (i, 0, 0)),
            tile,
            pl.BlockSpec((None, MLA_HEADS, ts, HEAD_PAD), lambda i, j: (i, 0, j, 0)),
        ] + [_const_spec(w.shape) for w in weights],
        out_specs=tile,
        out_shape=jax.ShapeDtypeStruct((b, s, d), F32),
        compiler_params=_params(2),
        name="merge_ln",
    )(x, mod_l, part, attn, *weights)


def _ffn_kernel(x_ref, mod_ref, wv_ref, wgt_ref, dwv_ref, dwg_ref, wd_ref, g_ref, b_ref, out_ref,
                hist_v, hist_g, bv0, bg0, bv1, bg1, h_ref, acc_ref, *, ts):
    s_idx = pl.program_id(1)

    @pl.when(s_idx == 0)
    def _():
        hist_v[...] = jnp.zeros(hist_v.shape, F32)
        hist_g[...] = jnp.zeros(hist_g.shape, F32)

    mod = mod_ref[...]
    h_ref[...] = (x_ref[...] * (1.0 + mod[4:5, :]) + mod[3:4, :]).astype(BF16)
    acc_ref[...] = jnp.zeros(acc_ref.shape, F32)

    def up(c, buf_v, buf_g):
        for w_ref, hist, buf in ((wv_ref, hist_v, buf_v), (wgt_ref, hist_g, buf_g)):
            buf[0:HIST_F, :] = hist[c]
            buf[HIST_F:HIST_F + ts, :] = _dot(h_ref[...], w_ref[c])
            hist[c] = buf[ts:ts + HIST_F, :]

    def taps(dw_ref, buf, c):
        w = dw_ref[c]
        out = None
        for k in range(FFN_CONV_WIDTH):
            term = w[k:k + 1, :] * buf[pl.ds(HIST_F - (FFN_CONV_WIDTH - 1) + k, ts), :]
            out = term if out is None else out + term
        return out

    def down(c, buf_v, buf_g):
        act = _silu(taps(dwg_ref, buf_g, c)) * taps(dwv_ref, buf_v, c)
        acc_ref[...] += _dot(act.astype(BF16), wd_ref[c])

    up(0, bv0, bg0)

    def pair(j, carry):
        c0 = 2 * j
        up(c0 + 1, bv1, bg1)
        down(c0, bv0, bg0)
        up(c0 + 2, bv0, bg0)
        down(c0 + 1, bv1, bg1)
        return carry
    lax.fori_loop(0, (FFN_NCHUNK - 1) // 2, pair, 0)
    down(FFN_NCHUNK - 1, bv0, bg0)
    x = x_ref[...]
    out_ref[...] = _layer_norm(DEEPNORM_ALPHA * x + (1.0 + mod[5:6, :]) * acc_ref[...], g_ref[...], b_ref[...])


def _ffn(x, mod_l, lw, ts):
    b, s, d = x.shape
    weights = [lw["w_up_v"], lw["w_up_g"], lw["ffn_dw_v"], lw["ffn_dw_g"], lw["w_down"], lw["ln2_g"], lw["ln2_b"]]
    tile = pl.BlockSpec((None, ts, d), lambda i, j: (i, j, 0))
    return pl.pallas_call(
        functools.partial(_ffn_kernel, ts=ts),
        grid=(b, s // ts),
        in_specs=[tile, pl.BlockSpec((None, 6, d), lambda i, j: (i, 0, 0))]
        + [_const_spec(w.shape) for w in weights],
        out_specs=tile,
        out_shape=jax.ShapeDtypeStruct((b, s, d), F32),
        scratch_shapes=[
            pltpu.VMEM((FFN_NCHUNK, HIST_F, FFN_CHUNK), F32),
            pltpu.VMEM((FFN_NCHUNK, HIST_F, FFN_CHUNK), F32),
            pltpu.VMEM((HIST_F + ts, FFN_CHUNK), F32),
            pltpu.VMEM((HIST_F + ts, FFN_CHUNK), F32),
            pltpu.VMEM((HIST_F + ts, FFN_CHUNK), F32),
            pltpu.VMEM((HIST_F + ts, FFN_CHUNK), F32),
            pltpu.VMEM((ts, d), BF16),
            pltpu.VMEM((ts, d), F32),
        ],
        compiler_params=_params(2),
        name="conv_glu_ffn",
    )(x, mod_l, *weights)


def _rope_tables(positions):
    inv = 1.0 / (ROPE_THETA ** (jnp.arange(0, QK_ROPE_DIM, 2, dtype=F32) / QK_ROPE_DIM))
    ang = positions.astype(F32)[..., None] * inv
    cos, sin = jnp.cos(ang), jnp.sin(ang)
    lead = cos.shape[:-1]
    pad = HEAD_PAD - QK_NOPE_DIM - QK_ROPE_DIM
    cos_t = jnp.concatenate([jnp.ones(lead + (QK_NOPE_DIM,), F32), cos, cos, jnp.ones(lead + (pad,), F32)], -1)
    sin_t = jnp.concatenate([jnp.zeros(lead + (QK_NOPE_DIM,), F32), sin, sin, jnp.zeros(lead + (pad,), F32)], -1)
    return cos_t, sin_t


def _rope_columns(w):
    rows = w.shape[0]
    z_lo = jnp.zeros((rows, QK_NOPE_DIM), w.dtype)
    z_hi = jnp.zeros((rows, HEAD_PAD - QK_NOPE_DIM - QK_ROPE_DIM), w.dtype)
    x1, x2 = w[:, :ROPE_HALF], w[:, ROPE_HALF:]
    return jnp.concatenate([z_lo, x1, x2, z_hi], -1), jnp.concatenate([z_lo, -x2, x1, z_hi], -1)


def _layer_weights(l, p):
    w_in, b_in = p["w_in"][l], p["b_in"][l]
    row = lambda v: v.reshape(1, -1)
    lw = {}
    b_row = row(b_in)
    gate = lambda t, i: t[:, OFF_POOL + i * D_MODEL:OFF_POOL + (i + 1) * D_MODEL]
    local = lambda t: jnp.concatenate([t[:, :OFF_SC_X], t[:, OFF_K_ROPE:OFF_POOL]], -1)
    gates_abd = lambda t: jnp.concatenate([gate(t, 0), gate(t, 1), gate(t, 3)], -1)
    lw["w1"], lw["b1"] = local(w_in).astype(BF16), local(b_row)
    lw["wg"], lw["bg"] = gates_abd(w_in).astype(BF16), gates_abd(b_row)
    lw["wg_c"], lw["bg_c"] = gate(w_in, 2).astype(BF16), gate(b_row, 2)
    lw["conv_dw"] = p["conv_dw"][l]
    lw["conv_ln_g"], lw["conv_ln_b"] = row(p["conv_ln_g"][l]), row(p["conv_ln_b"][l])
    lw["w_conv_out"] = p["w_conv_out"][l].astype(BF16)
    lw["sc_dw"] = p["sc_dw"][l]
    lw["w_sc_out"] = p["w_sc_out"][l].astype(BF16)
    lw["w_pool"] = p["w_pool"][l].astype(BF16)
    lw["pool_scale"] = row(p["pool_scale"][l])
    lw["w_pool_out"] = p["w_pool_out"][l].astype(BF16)
    kr, kr_swap = _rope_columns(w_in[:, OFF_KV_LAT:OFF_K_ROPE])
    bkr, bkr_swap = _rope_columns(b_in[None, OFF_KV_LAT:OFF_K_ROPE])
    lw["w2"] = jnp.concatenate([w_in[:, OFF_SC_X:OFF_KV_LAT], kr, kr_swap], -1).astype(BF16)
    lw["b2"] = jnp.concatenate([row(b_in[OFF_SC_X:OFF_KV_LAT]), bkr, bkr_swap], -1)
    lw["q_norm_g"], lw["kv_norm_g"] = row(p["q_norm_g"][l]), row(p["kv_norm_g"][l])
    w_uq = p["w_uq"][l].reshape(Q_LORA_RANK, MLA_HEADS, QK_NOPE_DIM + QK_ROPE_DIM)
    zq = jnp.zeros((Q_LORA_RANK, MLA_HEADS, HEAD_PAD - QK_NOPE_DIM - QK_ROPE_DIM), F32)
    x1 = w_uq[..., QK_NOPE_DIM:QK_NOPE_DIM + ROPE_HALF]
    x2 = w_uq[..., QK_NOPE_DIM + ROPE_HALF:]
    lw["wq"] = jnp.concatenate([w_uq, zq], -1).reshape(Q_LORA_RANK, -1).astype(BF16)
    lw["wq_swap"] = jnp.concatenate([jnp.zeros_like(w_uq[..., :QK_NOPE_DIM]), -x2, x1, zq], -1).reshape(
        Q_LORA_RANK, -1).astype(BF16)
    w_ukv = p["w_ukv"][l].reshape(KV_LORA_RANK, MLA_HEADS, QK_NOPE_DIM + V_HEAD_DIM)
    zk = jnp.zeros((KV_LORA_RANK, MLA_HEADS, HEAD_PAD - QK_NOPE_DIM), F32)
    zv = jnp.zeros((KV_LORA_RANK, MLA_HEADS, HEAD_PAD - V_HEAD_DIM), F32)
    lw["wk"] = jnp.concatenate([w_ukv[..., :QK_NOPE_DIM], zk], -1).reshape(KV_LORA_RANK, -1).astype(BF16)
    lw["wv"] = jnp.concatenate([w_ukv[..., QK_NOPE_DIM:], zv], -1).reshape(KV_LORA_RANK, -1).astype(BF16)
    lw["v_one"] = jnp.tile((jnp.arange(HEAD_PAD) == V_HEAD_DIM).astype(F32), MLA_HEADS).reshape(1, -1)
    w_mo = p["w_mla_out"][l].reshape(MLA_HEADS, V_HEAD_DIM, D_MODEL)
    lw["w_mla_out"] = jnp.concatenate(
        [w_mo, jnp.zeros((MLA_HEADS, HEAD_PAD - V_HEAD_DIM, D_MODEL), F32)], 1).astype(BF16)
    lw["w_o"] = p["w_o"][l].astype(BF16)
    lw["ln1_g"], lw["ln1_b"] = row(p["ln1_g"][l]), row(p["ln1_b"][l])
    chunked = lambda w: jnp.moveaxis(w.reshape(w.shape[0], FFN_NCHUNK, FFN_CHUNK), 1, 0)
    w_up, dw = p["w_up"][l], p["ffn_dw"][l]
    lw["w_up_v"], lw["w_up_g"] = chunked(w_up[:, :D_FF]).astype(BF16), chunked(w_up[:, D_FF:]).astype(BF16)
    lw["ffn_dw_v"], lw["ffn_dw_g"] = chunked(dw[:, :D_FF]), chunked(dw[:, D_FF:])
    lw["w_down"] = p["w_down"][l].reshape(FFN_NCHUNK, FFN_CHUNK, D_MODEL).astype(BF16)
    lw["ln2_g"], lw["ln2_b"] = row(p["ln2_g"][l]), row(p["ln2_b"][l])
    return lw


def _tiles(s):
    ts = min(512, s)
    tq = min(1024, s)
    assert s % ts == 0 and s % tq == 0 and ts >= HIST_A and tq % SOFTMAX_ROWS == 0
    return ts, tq


def _forward(x, c, positions, p, ts, tq):
    cos_t, sin_t = _rope_tables(positions)
    mod = _modulation(c, p["w_ada"], p["b_ada"])
    for l in range(DEPTH):
        lw = _layer_weights(l, p)
        part = _local_mixers(x, mod[l], lw, ts)
        q, k, v = _mla_proj(x, mod[l], cos_t, sin_t, lw, ts)
        attn = _flash_attention(q, k, v, tq)
        x = _merge(x, mod[l], part, attn, lw, ts)
        x = _ffn(x, mod[l], lw, ts)
    return x


def kernel(x, c, positions, w_ada, b_ada, w_in, b_in, conv_dw, conv_ln_g, conv_ln_b, w_conv_out, sc_dw, w_sc_out, q_norm_g, w_uq, kv_norm_g, w_ukv, w_mla_out, w_pool, pool_scale, w_pool_out, w_o, ln1_g, ln1_b, w_up, ffn_dw, w_down, ln2_g, ln2_b):
    p = dict(w_ada=w_ada, b_ada=b_ada, w_in=w_in, b_in=b_in, conv_dw=conv_dw, conv_ln_g=conv_ln_g,
             conv_ln_b=conv_ln_b, w_conv_out=w_conv_out, sc_dw=sc_dw, w_sc_out=w_sc_out, q_norm_g=q_norm_g,
             w_uq=w_uq, kv_norm_g=kv_norm_g, w_ukv=w_ukv, w_mla_out=w_mla_out, w_pool=w_pool,
             pool_scale=pool_scale, w_pool_out=w_pool_out, w_o=w_o, ln1_g=ln1_g, ln1_b=ln1_b, w_up=w_up,
             ffn_dw=ffn_dw, w_down=w_down, ln2_g=ln2_g, ln2_b=ln2_b)
    ts, tq = _tiles(x.shape[1])
    return _forward(x, c, positions, p, ts, tq)
```
